```python
import math
import jax, jax.numpy as jnp
from jax import lax
import numpy as np

D_MODEL = 2048
BATCH = 4
SEQ = 8192
DEPTH = 1
DEC_BATCH = 8
DEC_SEQ = 4096
PAST_LEN = 128

MIX_WIDTH = D_MODEL
SSD_WIDTH = MIX_WIDTH // 2
S5_WIDTH = MIX_WIDTH - SSD_WIDTH
SSD_HEAD_DIM = 64
SSD_HEADS = SSD_WIDTH // SSD_HEAD_DIM
SSD_GROUPS = 4
SSD_STATE = 128
SSD_CONV = 3
SSD_CHUNK = 128
XBC_WIDTH = SSD_WIDTH + 2 * SSD_GROUPS * SSD_STATE
S5_GROUP_CH = 16
S5_GROUPS = S5_WIDTH // S5_GROUP_CH
S5_STATE = 64
IN_COLS = SSD_WIDTH + XBC_WIDTH + SSD_HEADS + S5_WIDTH
MEM_LEN = 256
MEM_HEADS = 4
MEM_HEAD_DIM = D_MODEL // MEM_HEADS
N_EXPERTS = 16
EXPERT_FF = D_MODEL
CAPACITY_FACTOR = 2
EPS = 1e-6

kernel_name = 'hybrid_ssd_s5_ec_encoder'


def _rmsnorm(x, g):
    xf = x.astype(jnp.float32)
    y = xf * lax.rsqrt(jnp.mean(xf * xf, axis=-1, keepdims=True) + EPS) * g.astype(jnp.float32)
    return y.astype(x.dtype)


def _depthwise_conv(x, w):
    return lax.conv_general_dilated(x, w[:, None, :].astype(x.dtype), (1,), 'SAME',
                                    dimension_numbers=('NWC', 'WIO', 'NWC'),
                                    feature_group_count=x.shape[-1])


def _ssd_chunked(x, dt, a, bm, cm):
    b, l, h, p = x.shape
    g, n = bm.shape[-2:]
    r = h // g
    q = SSD_CHUNK
    c = l // q
    xdt = (x * dt[..., None]).reshape(b, c, q, g, r, p)
    a_cs = jnp.cumsum((dt * a).reshape(b, c, q, g, r).transpose(0, 1, 3, 4, 2), axis=-1)
    lower = jnp.tril(jnp.ones((q, q), dtype=bool))
    seg = jnp.exp(jnp.where(lower, a_cs[..., :, None] - a_cs[..., None, :], -jnp.inf))
    bc = bm.reshape(b, c, q, g, n)
    cc = cm.reshape(b, c, q, g, n)
    cb = jnp.einsum('bclgn,bcsgn->bcgls', cc, bc)
    y_diag = jnp.einsum('bcgrls,bcsgrp->bclgrp', cb[:, :, :, None] * seg, xdt)
    decay_in = jnp.exp(a_cs[..., -1:] - a_cs).transpose(0, 1, 4, 2, 3)[..., None]
    chunk_states = jnp.einsum('bclgn,bclgrp->bcgrpn', bc, xdt * decay_in)
    chunk_decay = jnp.exp(a_cs[..., -1])

    def step(state, inp):
        dec, st = inp
        return state * dec[..., None, None] + st, state

    init = jnp.zeros((b, g, r, p, n), x.dtype)
    _, prev = lax.scan(step, init, (jnp.moveaxis(chunk_decay, 1, 0), jnp.moveaxis(chunk_states, 1, 0)))
    prev = jnp.moveaxis(prev, 0, 1)
    decay_out = jnp.exp(a_cs).transpose(0, 1, 4, 2, 3)[..., None]
    y_off = jnp.einsum('bclgn,bcgrpn->bclgrp', cc, prev) * decay_out
    return (y_diag + y_off).reshape(b, l, h, p)


def _ssd_mixer(z, xbc, dt_raw, conv_w, conv_b, dt_bias, a_log, d_skip, norm_g):
    b, l, _ = xbc.shape
    xbc = jax.nn.silu(_depthwise_conv(xbc, conv_w) + conv_b.astype(xbc.dtype))
    xs, bm, cm = jnp.split(xbc, [SSD_WIDTH, SSD_WIDTH + SSD_GROUPS * SSD_STATE], axis=-1)
    x32 = xs.reshape(b, l, SSD_HEADS, SSD_HEAD_DIM).astype(jnp.float32)
    bm = bm.reshape(b, l, SSD_GROUPS, SSD_STATE).astype(jnp.float32)
    cm = cm.reshape(b, l, SSD_GROUPS, SSD_STATE).astype(jnp.float32)
    dt32 = dt_raw.astype(jnp.float32)
    dt_f = jax.nn.softplus(dt32 + dt_bias[0].astype(jnp.float32))
    a_f = -jnp.exp(a_log[0].astype(jnp.float32))
    y = _ssd_chunked(x32, dt_f, a_f, bm, cm)
    dt_b = jax.nn.softplus(dt32 + dt_bias[1].astype(jnp.float32))
    a_b = -jnp.exp(a_log[1].astype(jnp.float32))
    flip = lambda t: jnp.flip(t, axis=1)
    y = y + flip(_ssd_chunked(flip(x32), flip(dt_b), a_b, flip(bm), flip(cm)))
    y = y + d_skip.astype(jnp.float32)[:, None] * x32
    y = y.reshape(b, l, SSD_WIDTH) * jax.nn.silu(z.astype(jnp.float32))
    yg = y.reshape(b, l, SSD_GROUPS, SSD_WIDTH // SSD_GROUPS)
    yg = yg * lax.rsqrt(jnp.mean(yg * yg, axis=-1, keepdims=True) + EPS)
    y = yg.reshape(b, l, SSD_WIDTH) * norm_g.astype(jnp.float32)
    return y.astype(z.dtype)


def _linear_combine(e1, e2):
    a1, b1 = e1
    a2, b2 = e2
    return a2 * a1, a2 * b1 + b2


def _s5_mixer(u, a_re, a_im, log_step, b_re, b_im, c_re, c_im, d_skip, w_glu, norm_g):
    b, l, _ = u.shape
    u32 = u.astype(jnp.float32)
    ug = u32.reshape(b, l, S5_GROUPS, S5_GROUP_CH)
    f32 = jnp.float32
    lam = lax.complex(a_re.astype(f32), a_im.astype(f32))
    delta = jnp.exp(log_step.astype(f32))[..., None]
    abar = jnp.exp(lam * delta)
    bbar = ((abar - 1.0) / lam)[..., None] * lax.complex(b_re.astype(f32), b_im.astype(f32))
    cmat = lax.complex(c_re.astype(f32), c_im.astype(f32))

    def one_seq(us):
        uc = us.astype(jnp.complex64)
        y = jnp.zeros(us.shape, f32)
        for direction in range(2):
            bu = jnp.einsum('gpc,lgc->lgp', bbar[direction], uc)
            a_seq = jnp.broadcast_to(abar[direction], bu.shape)
            _, states = lax.associative_scan(_linear_combine, (a_seq, bu), reverse=(direction == 1))
            y = y + jnp.real(jnp.einsum('gcp,lgp->lgc', cmat[direction], states))
        return y

    y = lax.map(one_seq, ug).reshape(b, l, S5_WIDTH) + d_skip.astype(f32) * u32
    y = jax.nn.gelu(y)
    y = y * jax.nn.sigmoid(y @ w_glu.astype(f32))
    y = _rmsnorm(y, norm_g)
    return y.astype(u.dtype)


def _mem_attention(hn, mem_n, w_q, w_k, w_v, w_o):
    b, l, _ = hn.shape
    m = mem_n.shape[1]
    q = (hn @ w_q).reshape(b, l, MEM_HEADS, MEM_HEAD_DIM)
    k = (mem_n @ w_k).reshape(b, m, MEM_HEADS, MEM_HEAD_DIM)
    v = (mem_n @ w_v).reshape(b, m, MEM_HEADS, MEM_HEAD_DIM)
    s = jnp.einsum('blhd,bmhd->bhlm', q, k).astype(jnp.float32) * (MEM_HEAD_DIM ** -0.5)
    pr = jax.nn.softmax(s, axis=-1).astype(v.dtype)
    o = jnp.einsum('bhlm,bmhd->blhd', pr, v).reshape(b, l, D_MODEL)
    return o @ w_o


def _expert_choice_moe(hn, w_router, w_gate, w_up, w_down):
    b, l, d = hn.shape
    n_tok = b * l
    cap = CAPACITY_FACTOR * n_tok // N_EXPERTS
    hf = hn.reshape(n_tok, d)
    probs = jax.nn.softmax((hf @ w_router).astype(jnp.float32), axis=-1)
    gates, idx = lax.top_k(probs.T, cap)

    def expert(args):
        wg, wu, wd, ix, gt = args
        xe = hf[ix]
        he = jax.nn.silu(xe @ wg) * (xe @ wu)
        return ((he @ wd) * gt[:, None]).astype(hf.dtype)

    ye = lax.map(expert, (w_gate, w_up, w_down, idx, gates))
    out = jnp.zeros_like(hf).at[idx.reshape(-1)].add(ye.reshape(-1, d))
    return out.reshape(b, l, d)


def _trunk(x, mem, w):
    for i in range(DEPTH):
        h = _rmsnorm(x, w['norm_mix'][i])
        proj = h @ w['w_in'][i]
        z, xbc, dt_raw, u = jnp.split(
            proj, [SSD_WIDTH, SSD_WIDTH + XBC_WIDTH, SSD_WIDTH + XBC_WIDTH + SSD_HEADS], axis=-1)
        y_ssd = _ssd_mixer(z, xbc, dt_raw, w['conv_w'][i], w['conv_b'][i], w['ssd_dt_bias'][i],
                           w['ssd_a_log'][i], w['ssd_d'][i], w['ssd_norm'][i])
        y_s5 = _s5_mixer(u, w['s5_a_re'][i], w['s5_a_im'][i], w['s5_log_step'][i], w['s5_b_re'][i],
                         w['s5_b_im'][i], w['s5_c_re'][i], w['s5_c_im'][i], w['s5_d'][i],
                         w['s5_w_glu'][i], w['s5_norm'][i])
        x = x + jnp.concatenate([y_ssd, y_s5], axis=-1) @ w['w_out'][i]
        x = x + _mem_attention(_rmsnorm(x, w['norm_attn'][i]), _rmsnorm(mem, w['norm_mem'][i]),
                               w['w_q'][i], w['w_k'][i], w['w_v'][i], w['w_o'][i])
        x = x + _expert_choice_moe(_rmsnorm(x, w['norm_ffn'][i]), w['w_router'][i],
                                   w['w_gate'][i], w['w_up'][i], w['w_down'][i])
    return _rmsnorm(x, w['norm_final'])


def setup_inputs(seed: int = 0) -> dict:
    key = jax.random.key(seed)
    ks = jax.random.split(key, 40)
    f32 = jnp.float32

    def nrm(k, shape, scale):
        return jax.random.normal(k, shape, f32) * scale

    def gain(k, shape):
        return 1.0 + 0.02 * jax.random.normal(k, shape, f32)

    dt0 = jnp.exp(jax.random.uniform(ks[8], (DEPTH, 2, SSD_HEADS), f32, math.log(1e-3), math.log(1e-1)))
    n_idx = jnp.arange(S5_STATE, dtype=f32)
    return {
        'x_prompt': nrm(ks[0], (BATCH, SEQ, D_MODEL), 1.0),
        'x_sample': nrm(ks[1], (DEC_BATCH, DEC_SEQ, D_MODEL), 1.0),
        'mem_prompt': nrm(ks[2], (BATCH, MEM_LEN, D_MODEL), 1.0),
        'mem_sample': nrm(ks[3], (DEC_BATCH, MEM_LEN, D_MODEL), 1.0),
        'norm_mix': gain(ks[4], (DEPTH, D_MODEL)),
        'w_in': nrm(ks[5], (DEPTH, D_MODEL, IN_COLS), D_MODEL ** -0.5),
        'conv_w': nrm(ks[6], (DEPTH, SSD_CONV, XBC_WIDTH), SSD_CONV ** -0.5),
        'conv_b': nrm(ks[7], (DEPTH, XBC_WIDTH), 0.01),
        'ssd_dt_bias': dt0 + jnp.log(-jnp.expm1(-dt0)),
        'ssd_a_log': jnp.log(jax.random.uniform(ks[9], (DEPTH, 2, SSD_HEADS), f32, 1.0, 16.0)),
        'ssd_d': gain(ks[10], (DEPTH, SSD_HEADS)),
        'ssd_norm': gain(ks[11], (DEPTH, SSD_WIDTH)),
        's5_a_re': -0.5 + 0.01 * jax.random.normal(ks[12], (DEPTH, 2, S5_GROUPS, S5_STATE), f32),
        's5_a_im': math.pi * n_idx + 0.01 * jax.random.normal(ks[13], (DEPTH, 2, S5_GROUPS, S5_STATE), f32),
        's5_log_step': jax.random.uniform(ks[14], (DEPTH, 2, S5_GROUPS), f32, math.log(1e-3), math.log(1e-1)),
        's5_b_re': nrm(ks[15], (DEPTH, 2, S5_GROUPS, S5_STATE, S5_GROUP_CH), (2 * S5_GROUP_CH) ** -0.5),
        's5_b_im': nrm(ks[16], (DEPTH, 2, S5_GROUPS, S5_STATE, S5_GROUP_CH), (2 * S5_GROUP_CH) ** -0.5),
        's5_c_re': nrm(ks[17], (DEPTH, 2, S5_GROUPS, S5_GROUP_CH, S5_STATE), (2 * S5_STATE) ** -0.5),
        's5_c_im': nrm(ks[18], (DEPTH, 2, S5_GROUPS, S5_GROUP_CH, S5_STATE), (2 * S5_STATE) ** -0.5),
        's5_d': nrm(ks[19], (DEPTH, S5_WIDTH), 1.0),
        's5_w_glu': nrm(ks[20], (DEPTH, S5_WIDTH, S5_WIDTH), S5_WIDTH ** -0.5),
        's5_norm': gain(ks[21], (DEPTH, S5_WIDTH)),
        'w_out': nrm(ks[22], (DEPTH, MIX_WIDTH, D_MODEL), MIX_WIDTH ** -0.5),
        'norm_attn': gain(ks[23], (DEPTH, D_MODEL)),
        'norm_mem': gain(ks[24], (DEPTH, D_MODEL)),
        'w_q': nrm(ks[25], (DEPTH, D_MODEL, D_MODEL), D_MODEL ** -0.5),
        'w_k': nrm(ks[26], (DEPTH, D_MODEL, D_MODEL), D_MODEL ** -0.5),
        'w_v': nrm(ks[27], (DEPTH, D_MODEL, D_MODEL), D_MODEL ** -0.5),
        'w_o': nrm(ks[28], (DEPTH, D_MODEL, D_MODEL), D_MODEL ** -0.5),
        'norm_ffn': gain(ks[29], (DEPTH, D_MODEL)),
        'w_router': nrm(ks[30], (DEPTH, D_MODEL, N_EXPERTS), D_MODEL ** -0.5),
        'w_gate': nrm(ks[31], (DEPTH, N_EXPERTS, D_MODEL, EXPERT_FF), D_MODEL ** -0.5),
        'w_up': nrm(ks[32], (DEPTH, N_EXPERTS, D_MODEL, EXPERT_FF), D_MODEL ** -0.5),
        'w_down': nrm(ks[33], (DEPTH, N_EXPERTS, EXPERT_FF, D_MODEL), EXPERT_FF ** -0.5),
        'norm_final': gain(ks[34], (D_MODEL,)),
    }


def reference(x_prompt, x_sample, mem_prompt, mem_sample, norm_mix, w_in, conv_w, conv_b,
              ssd_dt_bias, ssd_a_log, ssd_d, ssd_norm, s5_a_re, s5_a_im, s5_log_step, s5_b_re,
              s5_b_im, s5_c_re, s5_c_im, s5_d, s5_w_glu, s5_norm, w_out, norm_attn, norm_mem,
              w_q, w_k, w_v, w_o, norm_ffn, w_router, w_gate, w_up, w_down, norm_final):
    weights = dict(norm_mix=norm_mix, w_in=w_in, conv_w=conv_w, conv_b=conv_b,
                   ssd_dt_bias=ssd_dt_bias, ssd_a_log=ssd_a_log, ssd_d=ssd_d, ssd_norm=ssd_norm,
                   s5_a_re=s5_a_re, s5_a_im=s5_a_im, s5_log_step=s5_log_step, s5_b_re=s5_b_re,
                   s5_b_im=s5_b_im, s5_c_re=s5_c_re, s5_c_im=s5_c_im, s5_d=s5_d, s5_w_glu=s5_w_glu,
                   s5_norm=s5_norm, w_out=w_out, norm_attn=norm_attn, norm_mem=norm_mem,
                   w_q=w_q, w_k=w_k, w_v=w_v, w_o=w_o, norm_ffn=norm_ffn, w_router=w_router,
                   w_gate=w_gate, w_up=w_up, w_down=w_down, norm_final=norm_final)
    y_prompt = _trunk(x_prompt, mem_prompt, weights)
    y_sample = _trunk(x_sample, mem_sample, weights)
    return (y_prompt, y_sample)
```

```python
import functools

import jax
import jax.numpy as jnp
from jax import lax
from jax.experimental import pallas as pl
from jax.experimental.pallas import tpu as pltpu

F32 = jnp.float32
BF16 = jnp.bfloat16
I32 = jnp.int32
HIGHEST = lax.Precision.HIGHEST

EPS = 1e-6
SSD_GROUPS = 4
MEM_HEADS = 4
CAPACITY_FACTOR = 2
LANES = 128
HALO_ROWS = 16
SSD_CHUNK = 256
S5_CHUNK = 16
VMEM_LIMIT = 56 * 1024 * 1024


def _cparams(*sem):
    return pltpu.CompilerParams(dimension_semantics=sem, vmem_limit_bytes=VMEM_LIMIT)


def _tile(n, target):
    if n <= target:
        return n
    t = (target // LANES) * LANES
    while n % t:
        t -= LANES
    return t


def _rms(x, g):
    return x * lax.rsqrt(jnp.mean(x * x, axis=-1, keepdims=True) + EPS) * g


def _norm_mm_kernel(x_ref, g_ref, w_ref, o_ref, h_scr):
    @pl.when(pl.program_id(1) == 0)
    def _():
        h_scr[...] = _rms(x_ref[...].astype(F32), g_ref[...]).astype(BF16)

    o_ref[...] = jnp.dot(h_scr[...], w_ref[...], preferred_element_type=F32).astype(o_ref.dtype)


def norm_matmul(x, g, w, out_dtype, tm=1024, tn=1024):
    t, d = x.shape
    n = w.shape[1]
    tm = _tile(t, tm)
    tn = _tile(n, tn)
    return pl.pallas_call(
        _norm_mm_kernel,
        grid=(t // tm, n // tn),
        in_specs=[pl.BlockSpec((tm, d), lambda i, j: (i, 0)),
                  pl.BlockSpec((1, d), lambda i, j: (0, 0)),
                  pl.BlockSpec((d, tn), lambda i, j: (0, j))],
        out_specs=pl.BlockSpec((tm, tn), lambda i, j: (i, j)),
        out_shape=jax.ShapeDtypeStruct((t, n), out_dtype),
        scratch_shapes=[pltpu.VMEM((tm, d), BF16)],
        compiler_params=_cparams("parallel", "arbitrary"),
        name="norm_matmul",
    )(x, g.reshape(1, d).astype(F32), w)


def _mm_res_kernel(*refs, n_a):
    a_refs = refs[:n_a]
    w_refs = refs[n_a:2 * n_a]
    r_ref, o_ref = refs[2 * n_a], refs[2 * n_a + 1]
    acc = r_ref[...].astype(F32)
    for a_ref, w_ref in zip(a_refs, w_refs):
        acc = acc + jnp.dot(a_ref[...], w_ref[...], preferred_element_type=F32)
    o_ref[...] = acc


def matmul_residual(a_list, w_list, r, tm=1024, tn=1024):
    t, n = r.shape
    tm = _tile(t, tm)
    tn = _tile(n, tn)
    n_a = len(a_list)
    in_specs = [pl.BlockSpec((tm, a.shape[1]), lambda i, j: (i, 0)) for a in a_list]
    in_specs += [pl.BlockSpec((w.shape[0], tn), lambda i, j: (0, j)) for w in w_list]
    in_specs += [pl.BlockSpec((tm, tn), lambda i, j: (i, j))]
    return pl.pallas_call(
        functools.partial(_mm_res_kernel, n_a=n_a),
        grid=(t // tm, n // tn),
        in_specs=in_specs,
        out_specs=pl.BlockSpec((tm, tn), lambda i, j: (i, j)),
        out_shape=jax.ShapeDtypeStruct((t, n), F32),
        compiler_params=_cparams("parallel", "arbitrary"),
        name="matmul_residual",
    )(*a_list, *w_list, r)


def _pair_expand(v, h0, p):
    q = v.shape[0]
    lane = lax.broadcasted_iota(I32, (q, 2 * p), 1)
    return jnp.where(lane < p, v[:, h0:h0 + 1], v[:, h0 + 1:h0 + 2])


def _ssd_core(xs, bm, cm, dtr, bias, alog, state_ref, *, reverse, heads, hdim, groups, nstate):
    q = xs.shape[0]
    per_group = heads // groups
    assert per_group % 2 == 0 and 2 * hdim == LANES
    dt = jax.nn.softplus(dtr + bias)
    dta = dt * (-jnp.exp(alog))
    row = lax.broadcasted_iota(I32, (q, q), 0)
    col = lax.broadcasted_iota(I32, (q, q), 1)
    mask = (row <= col) if reverse else (row >= col)
    cs = jnp.dot(mask.astype(F32), dta, precision=HIGHEST, preferred_element_type=F32)
    tot = cs[0:1, :] if reverse else cs[q - 1:q, :]
    cs_t = cs.T
    e_out = jnp.exp(cs)
    e_in = jnp.exp(tot - cs)
    e_tot = jnp.exp(tot)
    lane_lo = lax.broadcasted_iota(I32, (q, 2 * hdim), 1) < hdim
    ys = []
    for g in range(groups):
        b_g = bm[:, g * nstate:(g + 1) * nstate]
        c_g = cm[:, g * nstate:(g + 1) * nstate].astype(BF16)
        cb = lax.dot_general(c_g, b_g.astype(BF16), (((1,), (1,)), ((), ())),
                             preferred_element_type=F32)
        st = state_ref[g]
        y_off = jnp.dot(c_g, st.astype(BF16), preferred_element_type=F32)
        x_in, dec = [], []
        for pr in range(per_group // 2):
            h0 = g * per_group + 2 * pr
            c0 = h0 * hdim
            xdt = xs[:, c0:c0 + 2 * hdim] * _pair_expand(dt, h0, hdim)
            xdt16 = xdt.astype(BF16)
            y_pair = []
            for k in range(2):
                h = h0 + k
                seg = jnp.exp(jnp.where(mask, cs[:, h:h + 1] - cs_t[h:h + 1, :], -jnp.inf))
                m = (cb * seg).astype(BF16)
                y_pair.append(jnp.dot(m, xdt16, preferred_element_type=F32))
            y_d = jnp.where(lane_lo, y_pair[0], y_pair[1])
            o0 = 2 * pr * hdim
            ys.append(y_d + y_off[:, o0:o0 + 2 * hdim] * _pair_expand(e_out, h0, hdim))
            x_in.append((xdt * _pair_expand(e_in, h0, hdim)).astype(BF16))
            dec.append(_pair_expand(e_tot, h0, hdim))
        x_in = jnp.concatenate(x_in, axis=1) if len(x_in) > 1 else x_in[0]
        dec = jnp.concatenate(dec, axis=1) if len(dec) > 1 else dec[0]
        state_ref[g] = st * dec + jnp.dot(b_g.T.astype(BF16), x_in, preferred_element_type=F32)
    return jnp.concatenate(ys, axis=1)


def _ssd_fwd_kernel(xc_ref, xp_ref, xn_ref, dt_ref, cw_ref, cb_ref, bias_ref, alog_ref,
                    yf_ref, xact_ref, state_ref, *, dims):
    c = pl.program_id(1)
    nc = pl.num_programs(1)

    @pl.when(c == 0)
    def _():
        state_ref[...] = jnp.zeros_like(state_ref)

    x = xc_ref[...].astype(F32)
    q = x.shape[0]
    prev_row = jnp.where(c > 0, xp_ref[HALO_ROWS - 1:HALO_ROWS, :].astype(F32), 0.0)
    next_row = jnp.where(c < nc - 1, xn_ref[0:1, :].astype(F32), 0.0)
    rows = lax.broadcasted_iota(I32, (q, 1), 0)
    x_dn = jnp.where(rows == 0, prev_row, pltpu.roll(x, 1, 0))
    x_up = jnp.where(rows == q - 1, next_row, pltpu.roll(x, q - 1, 0))
    xc = cw_ref[0:1, :] * x_dn + cw_ref[1:2, :] * x + cw_ref[2:3, :] * x_up + cb_ref[...]
    xa = xc * jax.nn.sigmoid(xc)
    xact_ref[...] = xa.astype(BF16)
    w = dims["heads"] * dims["hdim"]
    gn = dims["groups"] * dims["nstate"]
    y = _ssd_core(xa[:, :w], xa[:, w:w + gn], xa[:, w + gn:], dt_ref[...], bias_ref[...],
                  alog_ref[...], state_ref, reverse=False, **dims)
    yf_ref[...] = y.astype(yf_ref.dtype)


def _ssd_bwd_kernel(xa_ref, dt_ref, yf_ref, z_ref, bias_ref, alog_ref, dsk_ref, ng_ref,
                    y_ref, state_ref, *, dims):
    @pl.when(pl.program_id(1) == 0)
    def _():
        state_ref[...] = jnp.zeros_like(state_ref)

    xa = xa_ref[...].astype(F32)
    w = dims["heads"] * dims["hdim"]
    gn = dims["groups"] * dims["nstate"]
    xs = xa[:, :w]
    yb = _ssd_core(xs, xa[:, w:w + gn], xa[:, w + gn:], dt_ref[...], bias_ref[...],
                   alog_ref[...], state_ref, reverse=True, **dims)
    y = yf_ref[...].astype(F32) + yb + dsk_ref[...] * xs
    z = z_ref[...].astype(F32)
    y = y * (z * jax.nn.sigmoid(z))
    gw = w // dims["groups"]
    parts = []
    for g in range(dims["groups"]):
        yg = y[:, g * gw:(g + 1) * gw]
        parts.append(yg * lax.rsqrt(jnp.mean(yg * yg, axis=-1, keepdims=True) + EPS))
    y_ref[...] = (jnp.concatenate(parts, axis=1) * ng_ref[...]).astype(y_ref.dtype)


def _pad_lanes(v):
    v = v.astype(F32)
    return jnp.pad(v, [(0, 0)] * (v.ndim - 1) + [(0, LANES - v.shape[-1])])


def ssd_mixer(zxu, dt_raw, batch, seq, conv_w, conv_b, dt_bias, a_log, d_skip, norm_g, dims):
    t = zxu.shape[0]
    w = dims["heads"] * dims["hdim"]
    xbc_w = conv_w.shape[-1]
    assert xbc_w % w == 0
    q = min(SSD_CHUNK, seq)
    nc = seq // q
    hb = q // HALO_ROWS
    nhalo = t // HALO_ROWS
    state_shape = (dims["groups"], dims["nstate"], (dims["heads"] // dims["groups"]) * dims["hdim"])
    bias = _pad_lanes(dt_bias)
    alog = _pad_lanes(a_log)
    full = lambda n: pl.BlockSpec((1, n), lambda b, c: (0, 0))
    vec = full(LANES)

    yf, xact = pl.pallas_call(
        functools.partial(_ssd_fwd_kernel, dims=dims),
        grid=(batch, nc),
        in_specs=[pl.BlockSpec((q, xbc_w), lambda b, c: (b * nc + c, 0)),
                  pl.BlockSpec((HALO_ROWS, xbc_w), lambda b, c: (jnp.maximum((b * nc + c) * hb - 1, 0), 0)),
                  pl.BlockSpec((HALO_ROWS, xbc_w), lambda b, c: (jnp.minimum((b * nc + c + 1) * hb, nhalo - 1), 0)),
                  pl.BlockSpec((q, LANES), lambda b, c: (b * nc + c, 0)),
                  pl.BlockSpec((3, xbc_w), lambda b, c: (0, 0)),
                  full(xbc_w), vec, vec],
        out_specs=[pl.BlockSpec((q, w), lambda b, c: (b * nc + c, 0)),
                   pl.BlockSpec((q, xbc_w), lambda b, c: (b * nc + c, 0))],
        out_shape=[jax.ShapeDtypeStruct((t, w), BF16), jax.ShapeDtypeStruct((t, xbc_w), BF16)],
        scratch_shapes=[pltpu.VMEM(state_shape, F32)],
        compiler_params=_cparams("parallel", "arbitrary"),
        name="ssd_forward",
    )(zxu, zxu, zxu, dt_raw, conv_w.astype(F32), conv_b.reshape(1, xbc_w).astype(F32), bias[0:1], alog[0:1])

    rev = lambda b, c: (b * nc + nc - 1 - c, 0)
    z_blk = xbc_w // w
    return pl.pallas_call(
        functools.partial(_ssd_bwd_kernel, dims=dims),
        grid=(batch, nc),
        in_specs=[pl.BlockSpec((q, xbc_w), rev),
                  pl.BlockSpec((q, LANES), rev),
                  pl.BlockSpec((q, w), rev),
                  pl.BlockSpec((q, w), lambda b, c: (b * nc + nc - 1 - c, z_blk)),
                  vec, vec, full(w), full(w)],
        out_specs=pl.BlockSpec((q, w), rev),
        out_shape=jax.ShapeDtypeStruct((t, w), BF16),
        scratch_shapes=[pltpu.VMEM(state_shape, F32)],
        compiler_params=_cparams("parallel", "arbitrary"),
        name="ssd_backward",
    )(xact, dt_raw, yf, zxu, bias[1:2], alog[1:2],
      jnp.repeat(d_skip.astype(F32), dims["hdim"]).reshape(1, w), norm_g.reshape(1, w).astype(F32))


def _s5_tables(a_re, a_im, log_step, b_re, b_im, c_re, c_im):
    qs = S5_CHUNK
    a_re, a_im, b_re, b_im, c_re, c_im = (v.astype(F32) for v in (a_re, a_im, b_re, b_im, c_re, c_im))
    _, g, p = a_re.shape
    cg = b_re.shape[-1]
    delta = jnp.exp(log_step.astype(F32))[..., None]
    mag = jnp.exp(a_re * delta)
    ar = mag * jnp.cos(a_im * delta)
    ai = mag * jnp.sin(a_im * delta)
    den = a_re * a_re + a_im * a_im
    qr = ((ar - 1.0) * a_re + ai * a_im) / den
    qi = (ai * a_re - (ar - 1.0) * a_im) / den
    bbr = qr[..., None] * b_re - qi[..., None] * b_im
    bbi = qr[..., None] * b_im + qi[..., None] * b_re
    pr, pi = [jnp.ones_like(ar)], [jnp.zeros_like(ar)]
    for _ in range(qs):
        pr.append(pr[-1] * ar - pi[-1] * ai)
        pi.append(pr[-2] * ai + pi[-1] * ar)
    pr = jnp.stack(pr)
    pi = jnp.stack(pi)
    cpr = c_re[None] * pr[:, :, :, None, :] - c_im[None] * pi[:, :, :, None, :]
    cpi = c_re[None] * pi[:, :, :, None, :] + c_im[None] * pr[:, :, :, None, :]
    wr = pr[..., None] * bbr[None] - pi[..., None] * bbi[None]
    wi = pr[..., None] * bbi[None] + pi[..., None] * bbr[None]
    kern = (jnp.einsum("tdgcp,dgpe->tdgce", cpr[:qs], bbr, precision=HIGHEST)
            - jnp.einsum("tdgcp,dgpe->tdgce", cpi[:qs], bbi, precision=HIGHEST))
    ii = jnp.arange(qs)[:, None]
    jj = jnp.arange(qs)[None, :]
    lag = ii - jj
    t_f = jnp.where((lag >= 0)[:, :, None, None, None], kern[jnp.clip(lag, 0, qs - 1), 0], 0.0)
    t_b = jnp.where((lag <= 0)[:, :, None, None, None], kern[jnp.clip(-lag, 0, qs - 1), 1], 0.0)
    toep = (t_f + t_b).transpose(2, 1, 4, 0, 3).reshape(g, qs * cg, qs * cg)
    def c_rows(d, powers):
        re = cpr[powers, d].transpose(1, 3, 0, 2)
        im = -cpi[powers, d].transpose(1, 3, 0, 2)
        return jnp.concatenate([re, im], axis=1).reshape(g, 2 * p, qs * cg)
    w_out = jnp.concatenate([toep, c_rows(0, jnp.arange(1, qs + 1)), c_rows(1, qs - jnp.arange(qs))], axis=1)
    def b_cols(d, powers):
        re = wr[powers, d].transpose(1, 0, 3, 2)
        im = wi[powers, d].transpose(1, 0, 3, 2)
        return jnp.concatenate([re, im], axis=3).reshape(g, qs * cg, 2 * p)
    w_state = jnp.concatenate([b_cols(0, qs - 1 - jnp.arange(qs)), b_cols(1, jnp.arange(qs))], axis=2)

    def multipliers(d):
        mul = jnp.concatenate([pr[qs, d], pr[qs, d]], axis=-1).reshape(1, g * 2 * p)
        swp = jnp.concatenate([-pi[qs, d], pi[qs, d]], axis=-1).reshape(1, g * 2 * p)
        return mul, swp
    return w_state.astype(BF16), w_out.astype(BF16), multipliers(0) + multipliers(1)


def _s5_state_kernel(u_ref, w_ref, vf_ref, vb_ref):
    v = jnp.dot(u_ref[0], w_ref[0], preferred_element_type=F32)
    half = v.shape[1] // 2
    vf_ref[...] = v[:, :half]
    vb_ref[...] = v[:, half:]


def _s5_scan_kernel(v_ref, mul_ref, swp_ref, o_ref, s_scr, *, rows, steps, reverse):
    @pl.when(pl.program_id(1) == 0)
    def _():
        s_scr[...] = jnp.zeros_like(s_scr)

    s = s_scr[...]
    mul = mul_ref[...]
    swp = swp_ref[...]
    width = s.shape[1]
    re_lane = (lax.broadcasted_iota(I32, (1, width), 1) % LANES) < (LANES // 2)
    order = range(steps - 1, -1, -1) if reverse else range(steps)
    for k in order:
        o_ref[k * rows:(k + 1) * rows, :] = s
        partner = jnp.where(re_lane, pltpu.roll(s, width - LANES // 2, 1), pltpu.roll(s, LANES // 2, 1))
        s = mul * s + swp * partner + v_ref[k * rows:(k + 1) * rows, :]
    s_scr[...] = s


def _s5_out_kernel(u_ref, sf_ref, sb_ref, w_ref, y_ref):
    qc = u_ref.shape[2]
    ns = sf_ref.shape[1]
    y = jnp.dot(u_ref[0], w_ref[0, :qc, :], preferred_element_type=F32)
    y = y + jnp.dot(sf_ref[...].astype(BF16), w_ref[0, qc:qc + ns, :], preferred_element_type=F32)
    y = y + jnp.dot(sb_ref[...].astype(BF16), w_ref[0, qc + ns:, :], preferred_element_type=F32)
    y_ref[0] = y.astype(y_ref.dtype)


def _s5_post_kernel(y_ref, u_ref, d_ref, w_ref, g_ref, o_ref):
    y = y_ref[...].astype(F32) + d_ref[...] * u_ref[...].astype(F32)
    y = jax.nn.gelu(y)
    gate = jax.nn.sigmoid(jnp.dot(y.astype(BF16), w_ref[...], preferred_element_type=F32))
    o_ref[...] = _rms(y * gate, g_ref[...]).astype(o_ref.dtype)


def s5_mixer(zxu, u_blk, batch, seq, tables, d_skip, w_glu, norm_g):
    w_state, w_out, (mul_f, swp_f, mul_b, swp_b) = tables
    t = zxu.shape[0]
    g, qc, ns2 = w_state.shape
    ns = ns2 // 2
    cg = qc // S5_CHUNK
    width = g * cg
    nchunk = seq // S5_CHUNK
    rows = nchunk * batch
    u = zxu[:, u_blk * width:(u_blk + 1) * width]
    ug = u.reshape(batch, nchunk, S5_CHUNK, g, cg).transpose(3, 1, 0, 2, 4).reshape(g, rows, qc)
    tr = _tile(rows, 1024)
    vf, vb = pl.pallas_call(
        _s5_state_kernel,
        grid=(g, rows // tr),
        in_specs=[pl.BlockSpec((1, tr, qc), lambda i, r: (i, r, 0)),
                  pl.BlockSpec((1, qc, ns2), lambda i, r: (i, 0, 0))],
        out_specs=[pl.BlockSpec((tr, ns), lambda i, r: (r, i))] * 2,
        out_shape=[jax.ShapeDtypeStruct((rows, g * ns), F32)] * 2,
        compiler_params=_cparams("parallel", "parallel"),
        name="s5_state_in",
    )(ug, w_state)

    panel = _tile(g * ns, 1024)
    steps = min(64, nchunk)
    nblk = nchunk // steps

    def scan(v, mul, swp, reverse):
        rmap = (lambda pnl, i: (nblk - 1 - i, pnl)) if reverse else (lambda pnl, i: (i, pnl))
        return pl.pallas_call(
            functools.partial(_s5_scan_kernel, rows=batch, steps=steps, reverse=reverse),
            grid=(g * ns // panel, nblk),
            in_specs=[pl.BlockSpec((steps * batch, panel), rmap),
                      pl.BlockSpec((1, panel), lambda pnl, i: (0, pnl)),
                      pl.BlockSpec((1, panel), lambda pnl, i: (0, pnl))],
            out_specs=pl.BlockSpec((steps * batch, panel), rmap),
            out_shape=jax.ShapeDtypeStruct(v.shape, F32),
            scratch_shapes=[pltpu.VMEM((batch, panel), F32)],
            compiler_params=_cparams("parallel", "arbitrary"),
            name="s5_scan_bwd" if reverse else "s5_scan_fwd",
        )(v, mul, swp)

    sf = scan(vf, mul_f, swp_f, False)
    sb = scan(vb, mul_b, swp_b, True)
    yg = pl.pallas_call(
        _s5_out_kernel,
        grid=(g, rows // tr),
        in_specs=[pl.BlockSpec((1, tr, qc), lambda i, r: (i, r, 0)),
                  pl.BlockSpec((tr, ns), lambda i, r: (r, i)),
                  pl.BlockSpec((tr, ns), lambda i, r: (r, i)),
                  pl.BlockSpec((1, qc + 2 * ns, qc), lambda i, r: (i, 0, 0))],
        out_specs=pl.BlockSpec((1, tr, qc), lambda i, r: (i, r, 0)),
        out_shape=jax.ShapeDtypeStruct((g, rows, qc), BF16),
        compiler_params=_cparams("parallel", "parallel"),
        name="s5_state_out",
    )(ug, sf, sb, w_out)
    y = yg.reshape(g, nchunk, batch, S5_CHUNK, cg).transpose(2, 1, 3, 0, 4).reshape(t, width)
    tm = _tile(t, 1024)
    return pl.pallas_call(
        _s5_post_kernel,
        grid=(t // tm,),
        in_specs=[pl.BlockSpec((tm, width), lambda i: (i, 0)),
                  pl.BlockSpec((tm, width), lambda i: (i, u_blk)),
                  pl.BlockSpec((1, width), lambda i: (0, 0)),
                  pl.BlockSpec((width, width), lambda i: (0, 0)),
                  pl.BlockSpec((1, width), lambda i: (0, 0))],
        out_specs=pl.BlockSpec((tm, width), lambda i: (i, 0)),
        out_shape=jax.ShapeDtypeStruct((t, width), BF16),
        compiler_params=_cparams("parallel"),
        name="s5_post",
    )(y, zxu, d_skip.reshape(1, width).astype(F32), w_glu.astype(BF16), norm_g.reshape(1, width).astype(F32))


def _attn_kernel(q_ref, kv_ref, o_ref, *, heads):
    d = q_ref.shape[1]
    hd = d // heads
    scale = hd ** -0.5
    for h in range(heads):
        qh = q_ref[:, h * hd:(h + 1) * hd]
        kh = kv_ref[:, h * hd:(h + 1) * hd]
        vh = kv_ref[:, d + h * hd:d + (h + 1) * hd]
        s = lax.dot_general(qh, kh, (((1,), (1,)), ((), ())), preferred_element_type=F32) * scale
        p = jnp.exp(s - jnp.max(s, axis=-1, keepdims=True))
        o = jnp.dot(p.astype(BF16), vh, preferred_element_type=F32) / jnp.sum(p, axis=-1, keepdims=True)
        o_ref[:, h * hd:(h + 1) * hd] = o.astype(o_ref.dtype)


def mem_attention(q, kv, batch, seq, mem_len):
    t, d = q.shape
    tq = _tile(seq, 1024)
    nq = seq // tq
    return pl.pallas_call(
        functools.partial(_attn_kernel, heads=MEM_HEADS),
        grid=(batch, nq),
        in_specs=[pl.BlockSpec((tq, d), lambda b, i: (b * nq + i, 0)),
                  pl.BlockSpec((mem_len, 2 * d), lambda b, i: (b, 0))],
        out_specs=pl.BlockSpec((tq, d), lambda b, i: (b * nq + i, 0)),
        out_shape=jax.ShapeDtypeStruct((t, d), BF16),
        compiler_params=_cparams("parallel", "arbitrary"),
        name="mem_attention",
    )(q, kv)


def _router_kernel(x_ref, g_ref, wr_ref, hn_ref, p_ref):
    hn = _rms(x_ref[...], g_ref[...])
    hn_ref[...] = hn.astype(hn_ref.dtype)
    logits = lax.dot_general(wr_ref[...], hn, (((1,), (1,)), ((), ())),
                             precision=HIGHEST, preferred_element_type=F32)
    e = jnp.exp(logits - jnp.max(logits, axis=0, keepdims=True))
    p_ref[...] = e / jnp.sum(e, axis=0, keepdims=True)


def router(x, g, w_router, tm=1024):
    t, d = x.shape
    e = w_router.shape[1]
    tm = _tile(t, tm)
    return pl.pallas_call(
        _router_kernel,
        grid=(t // tm,),
        in_specs=[pl.BlockSpec((tm, d), lambda i: (i, 0)),
                  pl.BlockSpec((1, d), lambda i: (0, 0)),
                  pl.BlockSpec((e, d), lambda i: (0, 0))],
        out_specs=[pl.BlockSpec((tm, d), lambda i: (i, 0)),
                   pl.BlockSpec((e, tm), lambda i: (0, i))],
        out_shape=[jax.ShapeDtypeStruct((t, d), BF16), jax.ShapeDtypeStruct((e, t), F32)],
        compiler_params=_cparams("parallel"),
        name="router",
    )(x, g.reshape(1, d).astype(F32), w_router.T.astype(F32))


def _select_kernel(p_ref, pos_ref, gate_ref, cnt_ref, *, cap, tb):
    e, t = p_ref.shape
    bits = pltpu.bitcast(p_ref[...], I32)

    def search(i, prefix):
        cand = prefix | jnp.left_shift(jnp.int32(1), 30 - i)
        cnt = jnp.sum((bits >= cand).astype(I32), axis=1, keepdims=True)
        return jnp.where(cnt >= cap, cand, prefix)

    thr = lax.fori_loop(0, 31, search, jnp.zeros((e, 1), I32))
    need = (cap - jnp.sum((bits > thr).astype(I32), axis=1, keepdims=True)).astype(F32)
    r = lax.broadcasted_iota(I32, (tb, tb), 0)
    c = lax.broadcasted_iota(I32, (tb, tb), 1)
    incl = (r <= c).astype(BF16)
    strict = (r < c).astype(BF16)

    def block(k, carry):
        carry_eq, carry_sel = carry
        sl = pl.ds(pl.multiple_of(k * tb, tb), tb)
        p = p_ref[:, sl]
        b = pltpu.bitcast(p, I32)
        eq = b == thr
        eq16 = jnp.where(eq, 1.0, 0.0).astype(BF16)
        rank = jnp.dot(eq16, incl, preferred_element_type=F32) + carry_eq
        sel = (b > thr) | (eq & (rank <= need))
        sel16 = jnp.where(sel, 1.0, 0.0).astype(BF16)
        pos = jnp.dot(sel16, strict, preferred_element_type=F32) + carry_sel
        pos_ref[:, sl] = jnp.where(sel, pos, -1.0)
        gate_ref[:, sl] = jnp.where(sel, p, 0.0)
        cnt_ref[k] = jnp.broadcast_to(carry_sel, (e, LANES))
        return (carry_eq + jnp.sum(eq16.astype(F32), axis=1, keepdims=True),
                carry_sel + jnp.sum(sel16.astype(F32), axis=1, keepdims=True))

    lax.fori_loop(0, t // tb, block, (jnp.zeros((e, 1), F32), jnp.zeros((e, 1), F32)))


def select_topk(probs_t, cap, tb):
    e, t = probs_t.shape
    nb = t // tb
    return pl.pallas_call(
        functools.partial(_select_kernel, cap=cap, tb=tb),
        grid=(1,),
        in_specs=[pl.BlockSpec((e, t), lambda i: (0, 0))],
        out_specs=[pl.BlockSpec((e, t), lambda i: (0, 0)),
                   pl.BlockSpec((e, t), lambda i: (0, 0)),
                   pl.BlockSpec((nb, e, LANES), lambda i: (0, 0, 0))],
        out_shape=[jax.ShapeDtypeStruct((e, t), F32), jax.ShapeDtypeStruct((e, t), F32),
                   jax.ShapeDtypeStruct((nb, e, LANES), F32)],
        compiler_params=_cparams("arbitrary"),
        name="select_topk",
    )(probs_t)


def _work_items(bounds, ts, nj, token_major, max_items):
    e, nkb1 = bounds.shape
    nkb = nkb1 - 1
    c0 = bounds[:, None, :-1]
    c1 = bounds[:, None, 1:]
    j = jnp.arange(nj, dtype=I32)[None, :, None]
    valid = (c1 > c0) & (c0 < (j + 1) * ts) & (c1 > j * ts)
    if token_major:
        jstar = jnp.clip(bounds[0, :-1] // ts, 0, nj - 1)
        forced = (jnp.arange(nj, dtype=I32)[:, None] == jstar[None, :])[None]
        valid = valid | (forced & (jnp.arange(e)[:, None, None] == 0))
        flat = valid.transpose(2, 0, 1).reshape(-1)
    else:
        flat = valid.reshape(-1)
    n = jnp.sum(flat.astype(I32))
    idx = jnp.nonzero(flat, size=max_items, fill_value=0)[0].astype(I32)
    it = jnp.arange(max_items, dtype=I32)
    real = it < n
    idx = jnp.where(real, idx, idx[jnp.maximum(n - 1, 0)])
    if token_major:
        kb, ex, sj = idx // (e * nj), (idx // nj) % e, idx % nj
        key = kb
    else:
        ex, sj, kb = idx // (nj * nkb), (idx // nkb) % nj, idx % nkb
        key = ex * nj + sj
    prev = jnp.concatenate([jnp.full((1,), -1, I32), key[:-1]])
    nxt = jnp.concatenate([key[1:], jnp.full((1,), -1, I32)])
    first = real & (key != prev)
    last = real & ((key != nxt) | (it == n - 1))
    return ex, sj, kb, first.astype(I32), last.astype(I32), real.astype(I32)


def _gather_kernel(ex_ref, sj_ref, kb_ref, first_ref, real_ref, pos_ref, hn_ref, xe_ref, acc_ref, *, ts):
    i = pl.program_id(0)

    @pl.when(first_ref[i] == 1)
    def _():
        acc_ref[...] = jnp.zeros_like(acc_ref)

    @pl.when(real_ref[i] == 1)
    def _():
        tb = hn_ref.shape[0]
        slot = pos_ref[pl.ds(ex_ref[i], 1), :] - (sj_ref[i] * ts).astype(F32)
        onehot = slot == lax.broadcasted_iota(I32, (ts, tb), 0).astype(F32)
        acc_ref[...] += jnp.dot(jnp.where(onehot, 1.0, 0.0).astype(BF16), hn_ref[...],
                                preferred_element_type=F32)

    xe_ref[...] = acc_ref[...].astype(xe_ref.dtype)


def moe_gather(hn, pos_t, items, cap, ts, tb):
    t, d = hn.shape
    e = pos_t.shape[0]
    nj = cap // ts
    ex, sj, kb, first, _, real = items
    grid_spec = pltpu.PrefetchScalarGridSpec(
        num_scalar_prefetch=5,
        grid=(ex.shape[0],),
        in_specs=[pl.BlockSpec((e, tb), lambda i, ex, sj, kb, f, r: (0, kb[i])),
                  pl.BlockSpec((tb, d), lambda i, ex, sj, kb, f, r: (kb[i], 0))],
        out_specs=pl.BlockSpec((ts, d), lambda i, ex, sj, kb, f, r: (ex[i] * nj + sj[i], 0)),
        scratch_shapes=[pltpu.VMEM((ts, d), F32)],
    )
    return pl.pallas_call(
        functools.partial(_gather_kernel, ts=ts),
        grid_spec=grid_spec,
        out_shape=jax.ShapeDtypeStruct((e * cap, d), BF16),
        compiler_params=_cparams("arbitrary"),
        name="moe_gather",
    )(ex, sj, kb, first, real, pos_t, hn)


def _ffn_kernel(xe_ref, wg_ref, wu_ref, wd_ref, ye_ref, acc_ref):
    f = pl.program_id(2)

    @pl.when(f == 0)
    def _():
        acc_ref[...] = jnp.zeros_like(acc_ref)

    xe = xe_ref[...]
    gate = jnp.dot(xe, wg_ref[0], preferred_element_type=F32)
    up = jnp.dot(xe, wu_ref[0], preferred_element_type=F32)
    he = (gate * jax.nn.sigmoid(gate) * up).astype(BF16)
    acc_ref[...] += jnp.dot(he, wd_ref[0], preferred_element_type=F32)

    @pl.when(f == pl.num_programs(2) - 1)
    def _():
        ye_ref[...] = acc_ref[...].astype(ye_ref.dtype)


def moe_ffn(xe, w_gate, w_up, w_down, cap, ts=1024, tf=512):
    n, d = xe.shape
    e, _, ff = w_gate.shape
    ts = _tile(cap, ts)
    tf = _tile(ff, tf)
    nsb = cap // ts
    return pl.pallas_call(
        _ffn_kernel,
        grid=(e, nsb, ff // tf),
        in_specs=[pl.BlockSpec((ts, d), lambda x, s, f: (x * nsb + s, 0)),
                  pl.BlockSpec((1, d, tf), lambda x, s, f: (x, 0, f)),
                  pl.BlockSpec((1, d, tf), lambda x, s, f: (x, 0, f)),
                  pl.BlockSpec((1, tf, d), lambda x, s, f: (x, f, 0))],
        out_specs=pl.BlockSpec((ts, d), lambda x, s, f: (x * nsb + s, 0)),
        out_shape=jax.ShapeDtypeStruct((n, d), BF16),
        scratch_shapes=[pltpu.VMEM((ts, d), F32)],
        compiler_params=_cparams("parallel", "parallel", "arbitrary"),
        name="moe_ffn",
    )(xe, w_gate, w_up, w_down)


def _scatter_kernel(ex_ref, sj_ref, kb_ref, first_ref, last_ref, real_ref,
                    pos_ref, gate_ref, ye_ref, x_ref, g_ref, y_ref, acc_ref, *, ts, final_norm):
    i = pl.program_id(0)

    @pl.when(first_ref[i] == 1)
    def _():
        acc_ref[...] = jnp.zeros_like(acc_ref)

    @pl.when(real_ref[i] == 1)
    def _():
        tb, e = pos_ref.shape
        mine = lax.broadcasted_iota(I32, (tb, e), 1) == ex_ref[i]
        pos = jnp.sum(jnp.where(mine, pos_ref[...], 0.0), axis=1, keepdims=True)
        gate = jnp.sum(jnp.where(mine, gate_ref[...], 0.0), axis=1, keepdims=True)
        slot = pos - (sj_ref[i] * ts).astype(F32)
        onehot = slot == lax.broadcasted_iota(I32, (tb, ts), 1).astype(F32)
        acc_ref[...] += gate * jnp.dot(jnp.where(onehot, 1.0, 0.0).astype(BF16), ye_ref[...],
                                       preferred_element_type=F32)

    @pl.when(last_ref[i] == 1)
    def _():
        y = x_ref[...] + acc_ref[...]
        y_ref[...] = _rms(y, g_ref[...]) if final_norm else y


def moe_scatter_norm(ye, pos_c, gate_c, x, g_final, final_norm, items, cap, ts, tb):
    t, d = x.shape
    e = pos_c.shape[1]
    nj = cap // ts
    ex, sj, kb, first, last, real = items
    tok = lambda i, ex, sj, kb, f, l, r: (kb[i], 0)
    grid_spec = pltpu.PrefetchScalarGridSpec(
        num_scalar_prefetch=6,
        grid=(ex.shape[0],),
        in_specs=[pl.BlockSpec((tb, e), tok),
                  pl.BlockSpec((tb, e), tok),
                  pl.BlockSpec((ts, d), lambda i, ex, sj, kb, f, l, r: (ex[i] * nj + sj[i], 0)),
                  pl.BlockSpec((tb, d), tok),
                  pl.BlockSpec((1, d), lambda i, ex, sj, kb, f, l, r: (0, 0))],
        out_specs=pl.BlockSpec((tb, d), tok),
        scratch_shapes=[pltpu.VMEM((tb, d), F32)],
    )
    return pl.pallas_call(
        functools.partial(_scatter_kernel, ts=ts, final_norm=final_norm),
        grid_spec=grid_spec,
        out_shape=jax.ShapeDtypeStruct((t, d), F32),
        compiler_params=_cparams("arbitrary"),
        name="moe_scatter_norm",
    )(ex, sj, kb, first, last, real, pos_c, gate_c, ye, x, g_final.reshape(1, d).astype(F32))


def expert_choice_moe_norm(x, norm_g, w_router, w_gate, w_up, w_down, g_final, final_norm):
    t, d = x.shape
    e = w_router.shape[1]
    cap = CAPACITY_FACTOR * t // e
    sel_tb = _tile(t, 512)
    hn, probs_t = router(x, norm_g, w_router)
    pos_t, gate_t, cnt = select_topk(probs_t, cap, sel_tb)
    counts = cnt[:, :, 0].T.astype(I32)
    bounds = jnp.concatenate([counts, jnp.full((e, 1), cap, I32)], axis=1)
    ts = _tile(cap, 256)
    nj = cap // ts
    g_tb = _tile(t, 1024)
    g_bounds = bounds[:, ::g_tb // sel_tb]
    g_items = _work_items(g_bounds, ts, nj, False, e * (nj + t // g_tb))
    xe = moe_gather(hn, pos_t, g_items, cap, ts, g_tb)
    ye = moe_ffn(xe, w_gate, w_up, w_down, cap)
    s_tb = sel_tb
    s_items = _work_items(bounds, ts, nj, True, e * (nj + t // s_tb) + t // s_tb)
    return moe_scatter_norm(ye, pos_t.T, gate_t.T, x, g_final, final_norm, s_items, cap, ts, s_tb)


def _prepare(p, depth_i):
    i = depth_i
    ssd_w = p["ssd_norm"].shape[-1]
    xbc_w = p["conv_w"].shape[-1]
    heads = p["ssd_a_log"].shape[-1]
    w_in = p["w_in"][i]
    o_z, o_x, o_dt = ssd_w, ssd_w + xbc_w, ssd_w + xbc_w + heads
    w_zxu = jnp.concatenate([w_in[:, o_z:o_x], w_in[:, :o_z], w_in[:, o_dt:]], axis=1).astype(BF16)
    w_dt = _pad_lanes(w_in[:, o_x:o_dt]).astype(BF16)
    dims = dict(heads=heads, hdim=ssd_w // heads, groups=SSD_GROUPS,
                nstate=(xbc_w - ssd_w) // (2 * SSD_GROUPS))
    tables = _s5_tables(p["s5_a_re"][i], p["s5_a_im"][i], p["s5_log_step"][i], p["s5_b_re"][i],
                        p["s5_b_im"][i], p["s5_c_re"][i], p["s5_c_im"][i])
    w_out = p["w_out"][i].astype(BF16)
    return dict(
        w_zxu=w_zxu, w_dt=w_dt, dims=dims, tables=tables, u_blk=(xbc_w + ssd_w) // (w_in.shape[1] - o_dt),
        w_out_ssd=w_out[:ssd_w], w_out_s5=w_out[ssd_w:],
        w_q=p["w_q"][i].astype(BF16),
        w_kv=jnp.concatenate([p["w_k"][i], p["w_v"][i]], axis=1).astype(BF16),
        w_o=p["w_o"][i].astype(BF16),
        w_gate=p["w_gate"][i].astype(BF16), w_up=p["w_up"][i].astype(BF16), w_down=p["w_down"][i].astype(BF16),
    )


def _trunk(x, mem, p, prepared):
    batch, seq, d = x.shape
    mem_len = mem.shape[1]
    t = batch * seq
    x = x.reshape(t, d)
    mem = mem.reshape(batch * mem_len, d)
    depth = p["w_in"].shape[0]
    for i in range(depth):
        w = prepared[i]
        zxu = norm_matmul(x, p["norm_mix"][i], w["w_zxu"], BF16)
        dt_raw = norm_matmul(x, p["norm_mix"][i], w["w_dt"], F32)
        y_ssd = ssd_mixer(zxu, dt_raw, batch, seq, p["conv_w"][i], p["conv_b"][i], p["ssd_dt_bias"][i],
                          p["ssd_a_log"][i], p["ssd_d"][i], p["ssd_norm"][i], w["dims"])
        y_s5 = s5_mixer(zxu, w["u_blk"], batch, seq, w["tables"], p["s5_d"][i], p["s5_w_glu"][i],
                        p["s5_norm"][i])
        x = matmul_residual([y_ssd, y_s5], [w["w_out_ssd"], w["w_out_s5"]], x)
        q = norm_matmul(x, p["norm_attn"][i], w["w_q"], BF16)
        kv = norm_matmul(mem, p["norm_mem"][i], w["w_kv"], BF16)
        o = mem_attention(q, kv, batch, seq, mem_len)
        x = matmul_residual([o], [w["w_o"]], x)
        x = expert_choice_moe_norm(x, p["norm_ffn"][i], p["w_router"][i], w["w_gate"], w["w_up"],
                                   w["w_down"], p["norm_final"], i == depth - 1)
    return x.reshape(batch, seq, d)


def kernel(x_prompt, x_sample, mem_prompt, mem_sample, norm_mix, w_in, conv_w, conv_b, ssd_dt_bias,
           ssd_a_log, ssd_d, ssd_norm, s5_a_re, s5_a_im, s5_log_step, s5_b_re, s5_b_im, s5_c_re,
           s5_c_im, s5_d, s5_w_glu, s5_norm, w_out, norm_attn, norm_mem, w_q, w_k, w_v, w_o,
           norm_ffn, w_router, w_gate, w_up, w_down, norm_final):
    p = dict(norm_mix=norm_mix, w_in=w_in, conv_w=conv_w, conv_b=conv_b, ssd_dt_bias=ssd_dt_bias,
             ssd_a_log=ssd_a_log, ssd_d=ssd_d, ssd_norm=ssd_norm, s5_a_re=s5_a_re, s5_a_im=s5_a_im,
             s5_log_step=s5_log_step, s5_b_re=s5_b_re, s5_b_im=s5_b_im, s5_c_re=s5_c_re,
             s5_c_im=s5_c_im, s5_d=s5_d, s5_w_glu=s5_w_glu, s5_norm=s5_norm, w_out=w_out,
             norm_attn=norm_attn, norm_mem=norm_mem, w_q=w_q, w_k=w_k, w_v=w_v, w_o=w_o,
             norm_ffn=norm_ffn, w_router=w_router, w_gate=w_gate, w_up=w_up, w_down=w_down,
             norm_final=norm_final)
    prepared = [_prepare(p, i) for i in range(w_in.shape[0])]
    return (_trunk(x_prompt, mem_prompt, p, prepared), _trunk(x_sample, mem_sample, p, prepared))
```

```python
import functools

import jax
import jax.numpy as jnp
from jax import lax
from jax.experimental import pallas as pl
from jax.experimental.pallas import tpu as pltpu

F32 = jnp.float32
BF16 = jnp.bfloat16
I32 = jnp.int32
HIGHEST = lax.Precision.HIGHEST

EPS = 1e-6
SSD_GROUPS = 4
MEM_HEADS = 4
CAPACITY_FACTOR = 2
LANES = 128
HALO_ROWS = 16
SSD_CHUNK = 256
S5_CHUNK = 16
VMEM_LIMIT = 56 * 1024 * 1024


def _cparams(*sem):
    return pltpu.CompilerParams(dimension_semantics=sem, vmem_limit_bytes=VMEM_LIMIT)


def _tile(n, target):
    if n <= target:
        return n
    t = (target // LANES) * LANES
    while n % t:
        t -= LANES
    return t


def _rms(x, g):
    return x * lax.rsqrt(jnp.mean(x * x, axis=-1, keepdims=True) + EPS) * g


def _norm_mm_kernel(x_ref, g_ref, w_ref, o_ref, h_scr):
    @pl.when(pl.program_id(1) == 0)
    def _():
        h_scr[...] = _rms(x_ref[...].astype(F32), g_ref[...]).astype(BF16)

    o_ref[...] = jnp.dot(h_scr[...], w_ref[...], preferred_element_type=F32).astype(o_ref.dtype)


def norm_matmul(x, g, w, out_dtype, tm=1024, tn=1024):
    t, d = x.shape
    n = w.shape[1]
    tm = _tile(t, tm)
    tn = _tile(n, tn)
    return pl.pallas_call(
        _norm_mm_kernel,
        grid=(t // tm, n // tn),
        in_specs=[pl.BlockSpec((tm, d), lambda i, j: (i, 0)),
                  pl.BlockSpec((1, d), lambda i, j: (0, 0)),
                  pl.BlockSpec((d, tn), lambda i, j: (0, j))],
        out_specs=pl.BlockSpec((tm, tn), lambda i, j: (i, j)),
        out_shape=jax.ShapeDtypeStruct((t, n), out_dtype),
        scratch_shapes=[pltpu.VMEM((tm, d), BF16)],
        compiler_params=_cparams("parallel", "arbitrary"),
        name="norm_matmul",
    )(x, g.reshape(1, d).astype(F32), w)


def _mm_res_kernel(*refs, n_a):
    a_refs = refs[:n_a]
    w_refs = refs[n_a:2 * n_a]
    r_ref, o_ref = refs[2 * n_a], refs[2 * n_a + 1]
    acc = r_ref[...].astype(F32)
    for a_ref, w_ref in zip(a_refs, w_refs):
        acc = acc + jnp.dot(a_ref[...], w_ref[...], preferred_element_type=F32)
    o_ref[...] = acc


def matmul_residual(a_list, w_list, r, tm=1024, tn=1024):
    t, n = r.shape
    tm = _tile(t, tm)
    tn = _tile(n, tn)
    n_a = len(a_list)
    in_specs = [pl.BlockSpec((tm, a.shape[1]), lambda i, j: (i, 0)) for a in a_list]
    in_specs += [pl.BlockSpec((w.shape[0], tn), lambda i, j: (0, j)) for w in w_list]
    in_specs += [pl.BlockSpec((tm, tn), lambda i, j: (i, j))]
    return pl.pallas_call(
        functools.partial(_mm_res_kernel, n_a=n_a),
        grid=(t // tm, n // tn),
        in_specs=in_specs,
        out_specs=pl.BlockSpec((tm, tn), lambda i, j: (i, j)),
        out_shape=jax.ShapeDtypeStruct((t, n), F32),
        compiler_params=_cparams("parallel", "arbitrary"),
        name="matmul_residual",
    )(*a_list, *w_list, r)


def _pair_expand(v, h0, p):
    q = v.shape[0]
    lane = lax.broadcasted_iota(I32, (q, 2 * p), 1)
    return jnp.where(lane < p, v[:, h0:h0 + 1], v[:, h0 + 1:h0 + 2])


def _ssd_core(xs, bm, cm, dtr, bias, alog, state_ref, *, reverse, heads, hdim, groups, nstate):
    q = xs.shape[0]
    per_group = heads // groups
    assert per_group % 2 == 0 and 2 * hdim == LANES
    dt = jax.nn.softplus(dtr + bias)
    dta = dt * (-jnp.exp(alog))
    row = lax.broadcasted_iota(I32, (q, q), 0)
    col = lax.broadcasted_iota(I32, (q, q), 1)
    mask = (row <= col) if reverse else (row >= col)
    cs = jnp.dot(mask.astype(F32), dta, precision=HIGHEST, preferred_element_type=F32)
    tot = cs[0:1, :] if reverse else cs[q - 1:q, :]
    cs_t = cs.T
    e_out = jnp.exp(cs)
    e_in = jnp.exp(tot - cs)
    e_tot = jnp.exp(tot)
    lane_lo = lax.broadcasted_iota(I32, (q, 2 * hdim), 1) < hdim
    ys = []
    for g in range(groups):
        b_g = bm[:, g * nstate:(g + 1) * nstate]
        c_g = cm[:, g * nstate:(g + 1) * nstate].astype(BF16)
        cb = lax.dot_general(c_g, b_g.astype(BF16), (((1,), (1,)), ((), ())),
                             preferred_element_type=F32)
        st = state_ref[g]
        y_off = jnp.dot(c_g, st.astype(BF16), preferred_element_type=F32)
        x_in, dec = [], []
        for pr in range(per_group // 2):
            h0 = g * per_group + 2 * pr
            c0 = h0 * hdim
            xdt = xs[:, c0:c0 + 2 * hdim] * _pair_expand(dt, h0, hdim)
            xdt16 = xdt.astype(BF16)
            y_pair = []
            for k in range(2):
                h = h0 + k
                seg = jnp.exp(jnp.where(mask, cs[:, h:h + 1] - cs_t[h:h + 1, :], -jnp.inf))
                m = (cb * seg).astype(BF16)
                y_pair.append(jnp.dot(m, xdt16, preferred_element_type=F32))
            y_d = jnp.where(lane_lo, y_pair[0], y_pair[1])
            o0 = 2 * pr * hdim
            ys.append(y_d + y_off[:, o0:o0 + 2 * hdim] * _pair_expand(e_out, h0, hdim))
            x_in.append((xdt * _pair_expand(e_in, h0, hdim)).astype(BF16))
            dec.append(_pair_expand(e_tot, h0, hdim))
        x_in = jnp.concatenate(x_in, axis=1) if len(x_in) > 1 else x_in[0]
        dec = jnp.concatenate(dec, axis=1) if len(dec) > 1 else dec[0]
        state_ref[g] = st * dec + jnp.dot(b_g.T.astype(BF16), x_in, preferred_element_type=F32)
    return jnp.concatenate(ys, axis=1)


def _ssd_fwd_kernel(xc_ref, xp_ref, xn_ref, dt_ref, cw_ref, cb_ref, bias_ref, alog_ref,
                    yf_ref, xact_ref, state_ref, *, dims):
    c = pl.program_id(1)
    nc = pl.num_programs(1)

    @pl.when(c == 0)
    def _():
        state_ref[...] = jnp.zeros_like(state_ref)

    x = xc_ref[...].astype(F32)
    q = x.shape[0]
    prev_row = jnp.where(c > 0, xp_ref[HALO_ROWS - 1:HALO_ROWS, :].astype(F32), 0.0)
    next_row = jnp.where(c < nc - 1, xn_ref[0:1, :].astype(F32), 0.0)
    rows = lax.broadcasted_iota(I32, (q, 1), 0)
    x_dn = jnp.where(rows == 0, prev_row, pltpu.roll(x, 1, 0))
    x_up = jnp.where(rows == q - 1, next_row, pltpu.roll(x, q - 1, 0))
    xc = cw_ref[0:1, :] * x_dn + cw_ref[1:2, :] * x + cw_ref[2:3, :] * x_up + cb_ref[...]
    xa = xc * jax.nn.sigmoid(xc)
    xact_ref[...] = xa.astype(BF16)
    w = dims["heads"] * dims["hdim"]
    gn = dims["groups"] * dims["nstate"]
    y = _ssd_core(xa[:, :w], xa[:, w:w + gn], xa[:, w + gn:], dt_ref[...], bias_ref[...],
                  alog_ref[...], state_ref, reverse=False, **dims)
    yf_ref[...] = y.astype(yf_ref.dtype)


def _ssd_bwd_kernel(xa_ref, dt_ref, yf_ref, z_ref, bias_ref, alog_ref, dsk_ref, ng_ref,
                    y_ref, state_ref, *, dims):
    @pl.when(pl.program_id(1) == 0)
    def _():
        state_ref[...] = jnp.zeros_like(state_ref)

    xa = xa_ref[...].astype(F32)
    w = dims["heads"] * dims["hdim"]
    gn = dims["groups"] * dims["nstate"]
    xs = xa[:, :w]
    yb = _ssd_core(xs, xa[:, w:w + gn], xa[:, w + gn:], dt_ref[...], bias_ref[...],
                   alog_ref[...], state_ref, reverse=True, **dims)
    y = yf_ref[...].astype(F32) + yb + dsk_ref[...] * xs
    z = z_ref[...].astype(F32)
    y = y * (z * jax.nn.sigmoid(z))
    gw = w // dims["groups"]
    parts = []
    for g in range(dims["groups"]):
        yg = y[:, g * gw:(g + 1) * gw]
        parts.append(yg * lax.rsqrt(jnp.mean(yg * yg, axis=-1, keepdims=True) + EPS))
    y_ref[...] = (jnp.concatenate(parts, axis=1) * ng_ref[...]).astype(y_ref.dtype)


def _pad_lanes(v):
    v = v.astype(F32)
    return jnp.pad(v, [(0, 0)] * (v.ndim - 1) + [(0, LANES - v.shape[-1])])


def ssd_mixer(zxu, dt_raw, batch, seq, conv_w, conv_b, dt_bias, a_log, d_skip, norm_g, dims):
    t = zxu.shape[0]
    w = dims["heads"] * dims["hdim"]
    xbc_w = conv_w.shape[-1]
    assert xbc_w % w == 0
    q = min(SSD_CHUNK, seq)
    nc = seq // q
    hb = q // HALO_ROWS
    nhalo = t // HALO_ROWS
    state_shape = (dims["groups"], dims["nstate"], (dims["heads"] // dims["groups"]) * dims["hdim"])
    bias = _pad_lanes(dt_bias)
    alog = _pad_lanes(a_log)
    full = lambda n: pl.BlockSpec((1, n), lambda b, c: (0, 0))
    vec = full(LANES)

    yf, xact = pl.pallas_call(
        functools.partial(_ssd_fwd_kernel, dims=dims),
        grid=(batch, nc),
        in_specs=[pl.BlockSpec((q, xbc_w), lambda b, c: (b * nc + c, 0)),
                  pl.BlockSpec((HALO_ROWS, xbc_w), lambda b, c: (jnp.maximum((b * nc + c) * hb - 1, 0), 0)),
                  pl.BlockSpec((HALO_ROWS, xbc_w), lambda b, c: (jnp.minimum((b * nc + c + 1) * hb, nhalo - 1), 0)),
                  pl.BlockSpec((q, LANES), lambda b, c: (b * nc + c, 0)),
                  pl.BlockSpec((3, xbc_w), lambda b, c: (0, 0)),
                  full(xbc_w), vec, vec],
        out_specs=[pl.BlockSpec((q, w), lambda b, c: (b * nc + c, 0)),
                   pl.BlockSpec((q, xbc_w), lambda b, c: (b * nc + c, 0))],
        out_shape=[jax.ShapeDtypeStruct((t, w), BF16), jax.ShapeDtypeStruct((t, xbc_w), BF16)],
        scratch_shapes=[pltpu.VMEM(state_shape, F32)],
        compiler_params=_cparams("parallel", "arbitrary"),
        name="ssd_forward",
    )(zxu, zxu, zxu, dt_raw, conv_w.astype(F32), conv_b.reshape(1, xbc_w).astype(F32), bias[0:1], alog[0:1])

    rev = lambda b, c: (b * nc + nc - 1 - c, 0)
    z_blk = xbc_w // w
    return pl.pallas_call(
        functools.partial(_ssd_bwd_kernel, dims=dims),
        grid=(batch, nc),
        in_specs=[pl.BlockSpec((q, xbc_w), rev),
                  pl.BlockSpec((q, LANES), rev),
                  pl.BlockSpec((q, w), rev),
                  pl.BlockSpec((q, w), lambda b, c: (b * nc + nc - 1 - c, z_blk)),
                  vec, vec, full(w), full(w)],
        out_specs=pl.BlockSpec((q, w), rev),
        out_shape=jax.ShapeDtypeStruct((t, w), BF16),
        scratch_shapes=[pltpu.VMEM(state_shape, F32)],
        compiler_params=_cparams("parallel", "arbitrary"),
        name="ssd_backward",
    )(xact, dt_raw, yf, zxu, bias[1:2], alog[1:2],
      jnp.repeat(d_skip.astype(F32), dims["hdim"]).reshape(1, w), norm_g.reshape(1, w).astype(F32))


def _s5_tables(a_re, a_im, log_step, b_re, b_im, c_re, c_im):
    qs = S5_CHUNK
    a_re, a_im, b_re, b_im, c_re, c_im = (v.astype(F32) for v in (a_re, a_im, b_re, b_im, c_re, c_im))
    _, g, p = a_re.shape
    cg = b_re.shape[-1]
    delta = jnp.exp(log_step.astype(F32))[..., None]
    mag = jnp.exp(a_re * delta)
    ar = mag * jnp.cos(a_im * delta)
    ai = mag * jnp.sin(a_im * delta)
    den = a_re * a_re + a_im * a_im
    qr = ((ar - 1.0) * a_re + ai * a_im) / den
    qi = (ai * a_re - (ar - 1.0) * a_im) / den
    bbr = qr[..., None] * b_re - qi[..., None] * b_im
    bbi = qr[..., None] * b_im + qi[..., None] * b_re
    pr, pi = [jnp.ones_like(ar)], [jnp.zeros_like(ar)]
    for _ in range(qs):
        pr.append(pr[-1] * ar - pi[-1] * ai)
        pi.append(pr[-2] * ai + pi[-1] * ar)
    pr = jnp.stack(pr)
    pi = jnp.stack(pi)
    cpr = c_re[None] * pr[:, :, :, None, :] - c_im[None] * pi[:, :, :, None, :]
    cpi = c_re[None] * pi[:, :, :, None, :] + c_im[None] * pr[:, :, :, None, :]
    wr = pr[..., None] * bbr[None] - pi[..., None] * bbi[None]
    wi = pr[..., None] * bbi[None] + pi[..., None] * bbr[None]
    kern = (jnp.einsum("tdgcp,dgpe->tdgce", cpr[:qs], bbr, precision=HIGHEST)
            - jnp.einsum("tdgcp,dgpe->tdgce", cpi[:qs], bbi, precision=HIGHEST))
    ii = jnp.arange(qs)[:, None]
    jj = jnp.arange(qs)[None, :]
    lag = ii - jj
    t_f = jnp.where((lag >= 0)[:, :, None, None, None], kern[jnp.clip(lag, 0, qs - 1), 0], 0.0)
    t_b = jnp.where((lag <= 0)[:, :, None, None, None], kern[jnp.clip(-lag, 0, qs - 1), 1], 0.0)
    toep = (t_f + t_b).transpose(2, 1, 4, 0, 3).reshape(g, qs * cg, qs * cg)
    def c_rows(d, powers):
        re = cpr[powers, d].transpose(1, 3, 0, 2)
        im = -cpi[powers, d].transpose(1, 3, 0, 2)
        return jnp.concatenate([re, im], axis=1).reshape(g, 2 * p, qs * cg)
    w_out = jnp.concatenate([toep, c_rows(0, jnp.arange(1, qs + 1)), c_rows(1, qs - jnp.arange(qs))], axis=1)
    def b_cols(d, powers):
        re = wr[powers, d].transpose(1, 0, 3, 2)
        im = wi[powers, d].transpose(1, 0, 3, 2)
        return jnp.concatenate([re, im], axis=3).reshape(g, qs * cg, 2 * p)
    w_state = jnp.concatenate([b_cols(0, qs - 1 - jnp.arange(qs)), b_cols(1, jnp.arange(qs))], axis=2)

    def multipliers(d):
        mul = jnp.concatenate([pr[qs, d], pr[qs, d]], axis=-1).reshape(1, g * 2 * p)
        swp = jnp.concatenate([-pi[qs, d], pi[qs, d]], axis=-1).reshape(1, g * 2 * p)
        return mul, swp
    return w_state.astype(BF16), w_out.astype(BF16), multipliers(0) + multipliers(1)


def _s5_state_kernel(u_ref, w_ref, vf_ref, vb_ref):
    v = jnp.dot(u_ref[0], w_ref[0], preferred_element_type=F32)
    half = v.shape[1] // 2
    vf_ref[...] = v[:, :half]
    vb_ref[...] = v[:, half:]


def _s5_scan_kernel(v_ref, mul_ref, swp_ref, o_ref, s_scr, *, rows, steps, reverse):
    @pl.when(pl.program_id(1) == 0)
    def _():
        s_scr[...] = jnp.zeros_like(s_scr)

    s = s_scr[...]
    mul = mul_ref[...]
    swp = swp_ref[...]
    width = s.shape[1]
    re_lane = (lax.broadcasted_iota(I32, (1, width), 1) % LANES) < (LANES // 2)
    order = range(steps - 1, -1, -1) if reverse else range(steps)
    for k in order:
        o_ref[k * rows:(k + 1) * rows, :] = s
        partner = jnp.where(re_lane, pltpu.roll(s, width - LANES // 2, 1), pltpu.roll(s, LANES // 2, 1))
        s = mul * s + swp * partner + v_ref[k * rows:(k + 1) * rows, :]
    s_scr[...] = s


def _s5_out_kernel(u_ref, sf_ref, sb_ref, w_ref, y_ref):
    qc = u_ref.shape[2]
    ns = sf_ref.shape[1]
    y = jnp.dot(u_ref[0], w_ref[0, :qc, :], preferred_element_type=F32)
    y = y + jnp.dot(sf_ref[...].astype(BF16), w_ref[0, qc:qc + ns, :], preferred_element_type=F32)
    y = y + jnp.dot(sb_ref[...].astype(BF16), w_ref[0, qc + ns:, :], preferred_element_type=F32)
    y_ref[0] = y.astype(y_ref.dtype)


def _s5_post_kernel(y_ref, u_ref, d_ref, w_ref, g_ref, o_ref):
    y = y_ref[...].astype(F32) + d_ref[...] * u_ref[...].astype(F32)
    y = jax.nn.gelu(y)
    gate = jax.nn.sigmoid(jnp.dot(y.astype(BF16), w_ref[...], preferred_element_type=F32))
    o_ref[...] = _rms(y * gate, g_ref[...]).astype(o_ref.dtype)


def _group_lane_blocks(n, cg):
    return lax.broadcasted_iota(I32, (n, LANES), 1) // cg


def _s5_split_kernel(u_ref, o_ref, *, groups, cg):
    n = u_ref.shape[0]
    per_vreg = LANES // cg
    width = groups * cg
    blk = _group_lane_blocks(n, cg)
    for v in range(groups // per_vreg):
        rolled = {}
        for j in range(S5_CHUNK):
            src = u_ref[:, j * width + v * LANES:j * width + (v + 1) * LANES].astype(F32)
            for k in range(per_vreg):
                rolled[j, k] = src if k == 0 else pltpu.roll(src, k * cg, 1)
        for gl in range(per_vreg):
            for h in range(S5_CHUNK // per_vreg):
                acc = rolled[h * per_vreg, (-gl) % per_vreg]
                for jj in range(1, per_vreg):
                    acc = jnp.where(blk == jj, rolled[h * per_vreg + jj, (jj - gl) % per_vreg], acc)
                o_ref[v * per_vreg + gl, :, h * LANES:(h + 1) * LANES] = acc.astype(o_ref.dtype)


def _s5_merge_kernel(y_ref, o_ref, *, groups, cg):
    n = y_ref.shape[1]
    per_vreg = LANES // cg
    width = groups * cg
    blk = _group_lane_blocks(n, cg)
    for v in range(groups // per_vreg):
        rolled = {}
        for gl in range(per_vreg):
            for h in range(S5_CHUNK // per_vreg):
                src = y_ref[v * per_vreg + gl, :, h * LANES:(h + 1) * LANES].astype(F32)
                for k in range(per_vreg):
                    rolled[gl, h, k] = src if k == 0 else pltpu.roll(src, k * cg, 1)
        for i in range(S5_CHUNK):
            h, ii = divmod(i, per_vreg)
            acc = rolled[0, h, (-ii) % per_vreg]
            for gl in range(1, per_vreg):
                acc = jnp.where(blk == gl, rolled[gl, h, (gl - ii) % per_vreg], acc)
            o_ref[:, i * width + v * LANES:i * width + (v + 1) * LANES] = acc.astype(o_ref.dtype)


def s5_mixer(zxu, u_blk, batch, seq, tables, d_skip, w_glu, norm_g):
    w_state, w_out, (mul_f, swp_f, mul_b, swp_b) = tables
    t = zxu.shape[0]
    g, qc, ns2 = w_state.shape
    ns = ns2 // 2
    cg = qc // S5_CHUNK
    width = g * cg
    nchunk = seq // S5_CHUNK
    rows = nchunk * batch
    u = zxu[:, u_blk * width:(u_blk + 1) * width]
    u_rows = u.reshape(batch, nchunk, S5_CHUNK * width).transpose(1, 0, 2).reshape(rows, S5_CHUNK * width)
    rn = min(64, rows)
    ug = pl.pallas_call(
        functools.partial(_s5_split_kernel, groups=g, cg=cg),
        grid=(rows // rn,),
        in_specs=[pl.BlockSpec((rn, S5_CHUNK * width), lambda r: (r, 0))],
        out_specs=pl.BlockSpec((g, rn, qc), lambda r: (0, r, 0)),
        out_shape=jax.ShapeDtypeStruct((g, rows, qc), BF16),
        compiler_params=_cparams("parallel"),
        name="s5_split",
    )(u_rows)
    tr = _tile(rows, 1024)
    vf, vb = pl.pallas_call(
        _s5_state_kernel,
        grid=(g, rows // tr),
        in_specs=[pl.BlockSpec((1, tr, qc), lambda i, r: (i, r, 0)),
                  pl.BlockSpec((1, qc, ns2), lambda i, r: (i, 0, 0))],
        out_specs=[pl.BlockSpec((tr, ns), lambda i, r: (r, i))] * 2,
        out_shape=[jax.ShapeDtypeStruct((rows, g * ns), F32)] * 2,
        compiler_params=_cparams("parallel", "parallel"),
        name="s5_state_in",
    )(ug, w_state)

    panel = _tile(g * ns, 4096)
    steps = min(32, nchunk)
    nblk = nchunk // steps

    def scan(v, mul, swp, reverse):
        rmap = (lambda pnl, i: (nblk - 1 - i, pnl)) if reverse else (lambda pnl, i: (i, pnl))
        return pl.pallas_call(
            functools.partial(_s5_scan_kernel, rows=batch, steps=steps, reverse=reverse),
            grid=(g * ns // panel, nblk),
            in_specs=[pl.BlockSpec((steps * batch, panel), rmap),
                      pl.BlockSpec((1, panel), lambda pnl, i: (0, pnl)),
                      pl.BlockSpec((1, panel), lambda pnl, i: (0, pnl))],
            out_specs=pl.BlockSpec((steps * batch, panel), rmap),
            out_shape=jax.ShapeDtypeStruct(v.shape, F32),
            scratch_shapes=[pltpu.VMEM((batch, panel), F32)],
            compiler_params=_cparams("parallel", "arbitrary"),
            name="s5_scan_bwd" if reverse else "s5_scan_fwd",
        )(v, mul, swp)

    sf = scan(vf, mul_f, swp_f, False)
    sb = scan(vb, mul_b, swp_b, True)
    yg = pl.pallas_call(
        _s5_out_kernel,
        grid=(g, rows // tr),
        in_specs=[pl.BlockSpec((1, tr, qc), lambda i, r: (i, r, 0)),
                  pl.BlockSpec((tr, ns), lambda i, r: (r, i)),
                  pl.BlockSpec((tr, ns), lambda i, r: (r, i)),
                  pl.BlockSpec((1, qc + 2 * ns, qc), lambda i, r: (i, 0, 0))],
        out_specs=pl.BlockSpec((1, tr, qc), lambda i, r: (i, r, 0)),
        out_shape=jax.ShapeDtypeStruct((g, rows, qc), BF16),
        compiler_params=_cparams("parallel", "parallel"),
        name="s5_state_out",
    )(ug, sf, sb, w_out)
    y_rows = pl.pallas_call(
        functools.partial(_s5_merge_kernel, groups=g, cg=cg),
        grid=(rows // rn,),
        in_specs=[pl.BlockSpec((g, rn, qc), lambda r: (0, r, 0))],
        out_specs=pl.BlockSpec((rn, S5_CHUNK * width), lambda r: (r, 0)),
        out_shape=jax.ShapeDtypeStruct((rows, S5_CHUNK * width), BF16),
        compiler_params=_cparams("parallel"),
        name="s5_merge",
    )(yg)
    y = y_rows.reshape(nchunk, batch, S5_CHUNK * width).transpose(1, 0, 2).reshape(t, width)
    tm = _tile(t, 1024)
    return pl.pallas_call(
        _s5_post_kernel,
        grid=(t // tm,),
        in_specs=[pl.BlockSpec((tm, width), lambda i: (i, 0)),
                  pl.BlockSpec((tm, width), lambda i: (i, u_blk)),
                  pl.BlockSpec((1, width), lambda i: (0, 0)),
                  pl.BlockSpec((width, width), lambda i: (0, 0)),
                  pl.BlockSpec((1, width), lambda i: (0, 0))],
        out_specs=pl.BlockSpec((tm, width), lambda i: (i, 0)),
        out_shape=jax.ShapeDtypeStruct((t, width), BF16),
        compiler_params=_cparams("parallel"),
        name="s5_post",
    )(y, zxu, d_skip.reshape(1, width).astype(F32), w_glu.astype(BF16), norm_g.reshape(1, width).astype(F32))


def _attn_kernel(q_ref, kv_ref, o_ref, *, heads):
    d = q_ref.shape[1]
    hd = d // heads
    scale = hd ** -0.5
    for h in range(heads):
        qh = q_ref[:, h * hd:(h + 1) * hd]
        kh = kv_ref[:, h * hd:(h + 1) * hd]
        vh = kv_ref[:, d + h * hd:d + (h + 1) * hd]
        s = lax.dot_general(qh, kh, (((1,), (1,)), ((), ())), preferred_element_type=F32) * scale
        p = jnp.exp(s - jnp.max(s, axis=-1, keepdims=True))
        o = jnp.dot(p.astype(BF16), vh, preferred_element_type=F32) / jnp.sum(p, axis=-1, keepdims=True)
        o_ref[:, h * hd:(h + 1) * hd] = o.astype(o_ref.dtype)


def mem_attention(q, kv, batch, seq, mem_len):
    t, d = q.shape
    tq = _tile(seq, 1024)
    nq = seq // tq
    return pl.pallas_call(
        functools.partial(_attn_kernel, heads=MEM_HEADS),
        grid=(batch, nq),
        in_specs=[pl.BlockSpec((tq, d), lambda b, i: (b * nq + i, 0)),
                  pl.BlockSpec((mem_len, 2 * d), lambda b, i: (b, 0))],
        out_specs=pl.BlockSpec((tq, d), lambda b, i: (b * nq + i, 0)),
        out_shape=jax.ShapeDtypeStruct((t, d), BF16),
        compiler_params=_cparams("parallel", "arbitrary"),
        name="mem_attention",
    )(q, kv)


def _router_kernel(x_ref, g_ref, wr_ref, hn_ref, p_ref):
    hn = _rms(x_ref[...], g_ref[...])
    hn_ref[...] = hn.astype(hn_ref.dtype)
    logits = lax.dot_general(wr_ref[...], hn, (((1,), (1,)), ((), ())),
                             precision=HIGHEST, preferred_element_type=F32)
    e = jnp.exp(logits - jnp.max(logits, axis=0, keepdims=True))
    p_ref[...] = e / jnp.sum(e, axis=0, keepdims=True)


def router(x, g, w_router, tm=1024):
    t, d = x.shape
    e = w_router.shape[1]
    tm = _tile(t, tm)
    return pl.pallas_call(
        _router_kernel,
        grid=(t // tm,),
        in_specs=[pl.BlockSpec((tm, d), lambda i: (i, 0)),
                  pl.BlockSpec((1, d), lambda i: (0, 0)),
                  pl.BlockSpec((e, d), lambda i: (0, 0))],
        out_specs=[pl.BlockSpec((tm, d), lambda i: (i, 0)),
                   pl.BlockSpec((e, tm), lambda i: (0, i))],
        out_shape=[jax.ShapeDtypeStruct((t, d), BF16), jax.ShapeDtypeStruct((e, t), F32)],
        compiler_params=_cparams("parallel"),
        name="router",
    )(x, g.reshape(1, d).astype(F32), w_router.T.astype(F32))


def _select_kernel(p_ref, pos_ref, gate_ref, cnt_ref, *, cap, tb):
    e, t = p_ref.shape
    bits = pltpu.bitcast(p_ref[...], I32)

    def search(i, prefix):
        cand = prefix | jnp.left_shift(jnp.int32(1), 30 - i)
        cnt = jnp.sum((bits >= cand).astype(I32), axis=1, keepdims=True)
        return jnp.where(cnt >= cap, cand, prefix)

    thr = lax.fori_loop(0, 31, search, jnp.zeros((e, 1), I32))
    need = (cap - jnp.sum((bits > thr).astype(I32), axis=1, keepdims=True)).astype(F32)
    r = lax.broadcasted_iota(I32, (tb, tb), 0)
    c = lax.broadcasted_iota(I32, (tb, tb), 1)
    incl = (r <= c).astype(BF16)
    strict = (r < c).astype(BF16)

    def block(k, carry):
        carry_eq, carry_sel = carry
        sl = pl.ds(pl.multiple_of(k * tb, tb), tb)
        p = p_ref[:, sl]
        b = pltpu.bitcast(p, I32)
        eq = b == thr
        eq16 = jnp.where(eq, 1.0, 0.0).astype(BF16)
        rank = jnp.dot(eq16, incl, preferred_element_type=F32) + carry_eq
        sel = (b > thr) | (eq & (rank <= need))
        sel16 = jnp.where(sel, 1.0, 0.0).astype(BF16)
        pos = jnp.dot(sel16, strict, preferred_element_type=F32) + carry_sel
        pos_ref[:, sl] = jnp.where(sel, pos, -1.0)
        gate_ref[:, sl] = jnp.where(sel, p, 0.0)
        cnt_ref[k] = jnp.broadcast_to(carry_sel, (e, LANES))
        return (carry_eq + jnp.sum(eq16.astype(F32), axis=1, keepdims=True),
                carry_sel + jnp.sum(sel16.astype(F32), axis=1, keepdims=True))

    lax.fori_loop(0, t // tb, block, (jnp.zeros((e, 1), F32), jnp.zeros((e, 1), F32)))


def select_topk(probs_t, cap, tb):
    e, t = probs_t.shape
    nb = t // tb
    return pl.pallas_call(
        functools.partial(_select_kernel, cap=cap, tb=tb),
        grid=(1,),
        in_specs=[pl.BlockSpec((e, t), lambda i: (0, 0))],
        out_specs=[pl.BlockSpec((e, t), lambda i: (0, 0)),
                   pl.BlockSpec((e, t), lambda i: (0, 0)),
                   pl.BlockSpec((nb, e, LANES), lambda i: (0, 0, 0))],
        out_shape=[jax.ShapeDtypeStruct((e, t), F32), jax.ShapeDtypeStruct((e, t), F32),
                   jax.ShapeDtypeStruct((nb, e, LANES), F32)],
        compiler_params=_cparams("arbitrary"),
        name="select_topk",
    )(probs_t)


def _gather_items(bounds, ts, nj, max_items):
    e, nkb1 = bounds.shape
    nkb = nkb1 - 1
    c0 = bounds[:, None, :-1]
    c1 = bounds[:, None, 1:]
    j = jnp.arange(nj, dtype=I32)[None, :, None]
    flat = ((c1 > c0) & (c0 < (j + 1) * ts) & (c1 > j * ts)).reshape(-1)
    n = jnp.sum(flat.astype(I32))
    idx = jnp.nonzero(flat, size=max_items, fill_value=0)[0].astype(I32)
    it = jnp.arange(max_items, dtype=I32)
    real = it < n
    idx = jnp.where(real, idx, idx[jnp.maximum(n - 1, 0)])
    ex, sj, kb = idx // (nj * nkb), (idx // nkb) % nj, idx % nkb
    key = ex * nj + sj
    first = real & (key != jnp.concatenate([jnp.full((1,), -1, I32), key[:-1]]))
    return ex, sj, kb, first.astype(I32), real.astype(I32)


def _gather_kernel(ex_ref, sj_ref, kb_ref, first_ref, real_ref, pos_ref, hn_ref, xe_ref, acc_ref, *, ts):
    i = pl.program_id(0)

    @pl.when(first_ref[i] == 1)
    def _():
        acc_ref[...] = jnp.zeros_like(acc_ref)

    @pl.when(real_ref[i] == 1)
    def _():
        tb = hn_ref.shape[0]
        slot = pos_ref[pl.ds(ex_ref[i], 1), :] - (sj_ref[i] * ts).astype(F32)
        onehot = slot == lax.broadcasted_iota(I32, (ts, tb), 0).astype(F32)
        acc_ref[...] += jnp.dot(jnp.where(onehot, 1.0, 0.0).astype(BF16), hn_ref[...],
                                preferred_element_type=F32)

    xe_ref[...] = acc_ref[...].astype(xe_ref.dtype)


def moe_gather(hn, pos_t, items, cap, ts, tb):
    t, d = hn.shape
    e = pos_t.shape[0]
    nj = cap // ts
    ex, sj, kb, first, real = items
    grid_spec = pltpu.PrefetchScalarGridSpec(
        num_scalar_prefetch=5,
        grid=(ex.shape[0],),
        in_specs=[pl.BlockSpec((e, tb), lambda i, ex, sj, kb, f, r: (0, kb[i])),
                  pl.BlockSpec((tb, d), lambda i, ex, sj, kb, f, r: (kb[i], 0))],
        out_specs=pl.BlockSpec((ts, d), lambda i, ex, sj, kb, f, r: (ex[i] * nj + sj[i], 0)),
        scratch_shapes=[pltpu.VMEM((ts, d), F32)],
    )
    return pl.pallas_call(
        functools.partial(_gather_kernel, ts=ts),
        grid_spec=grid_spec,
        out_shape=jax.ShapeDtypeStruct((e * cap, d), BF16),
        compiler_params=_cparams("arbitrary"),
        name="moe_gather",
    )(ex, sj, kb, first, real, pos_t, hn)


def _ffn_kernel(xe_ref, wg_ref, wu_ref, wd_ref, ye_ref, acc_ref):
    f = pl.program_id(2)

    @pl.when(f == 0)
    def _():
        acc_ref[...] = jnp.zeros_like(acc_ref)

    xe = xe_ref[...]
    gate = jnp.dot(xe, wg_ref[0].astype(BF16), preferred_element_type=F32)
    up = jnp.dot(xe, wu_ref[0].astype(BF16), preferred_element_type=F32)
    he = (gate * jax.nn.sigmoid(gate) * up).astype(BF16)
    acc_ref[...] += jnp.dot(he, wd_ref[0].astype(BF16), preferred_element_type=F32)

    @pl.when(f == pl.num_programs(2) - 1)
    def _():
        ye_ref[...] = acc_ref[...].astype(ye_ref.dtype)


def moe_ffn(xe, w_gate, w_up, w_down, cap, ts=1024, tf=256):
    n, d = xe.shape
    e, _, ff = w_gate.shape
    ts = _tile(cap, ts)
    tf = _tile(ff, tf)
    nsb = cap // ts
    return pl.pallas_call(
        _ffn_kernel,
        grid=(e, nsb, ff // tf),
        in_specs=[pl.BlockSpec((ts, d), lambda x, s, f: (x * nsb + s, 0)),
                  pl.BlockSpec((1, d, tf), lambda x, s, f: (x, 0, f)),
                  pl.BlockSpec((1, d, tf), lambda x, s, f: (x, 0, f)),
                  pl.BlockSpec((1, tf, d), lambda x, s, f: (x, f, 0))],
        out_specs=pl.BlockSpec((ts, d), lambda x, s, f: (x * nsb + s, 0)),
        out_shape=jax.ShapeDtypeStruct((n, d), BF16),
        scratch_shapes=[pltpu.VMEM((ts, d), F32)],
        compiler_params=_cparams("parallel", "parallel", "arbitrary"),
        name="moe_ffn",
    )(xe, w_gate, w_up, w_down)


SLOT_SPLIT = 64
SCATTER_WINDOW = 128
SCATTER_WINDOWS = 4
ROW_ALIGN = 16
NO_SLOT = 1 << 24


def _scatter_kernel(kb_ref, first_ref, last_ref, real_ref, ex_ref, off_ref, base_ref, vlo_ref,
                    tok_ref, *rest, experts, final_norm):
    ye_refs = rest[:SCATTER_WINDOWS]
    x_ref, g_ref, y_ref, acc_ref = rest[SCATTER_WINDOWS:]
    i = pl.program_id(0)
    win = SCATTER_WINDOW

    @pl.when(first_ref[i] == 1)
    def _():
        acc_ref[...] = jnp.zeros_like(acc_ref)

    @pl.when(real_ref[i] == 1)
    def _():
        tok = tok_ref[...]
        kdim = tok.shape[1]
        e = experts
        row = lax.broadcasted_iota(I32, (kdim, 2 * win), 0)
        col = lax.broadcasted_iota(I32, (kdim, 2 * win), 1)
        lane = lax.broadcasted_iota(I32, (1, win), 1).astype(F32)
        pieces = []
        for k in range(SCATTER_WINDOWS):
            w = i * SCATTER_WINDOWS + k
            ek = ex_ref[w]
            sel = jnp.where((row == ek) & (col < win), float(SLOT_SPLIT), 0.0)
            sel = sel + jnp.where((row == e + ek) & (col < win), 1.0, 0.0)
            sel = sel + jnp.where((row == 2 * e + ek) & (col >= win), 1.0, 0.0)
            both = jnp.dot(tok, sel.astype(BF16), preferred_element_type=F32)
            slot = both[:, :win]
            hit = (slot - base_ref[w].astype(F32) == lane) & (slot >= vlo_ref[w].astype(F32))
            pieces.append(jnp.where(hit, both[:, win:], 0.0).astype(BF16))
        onehot = jnp.concatenate(pieces, axis=1)
        rows = jnp.concatenate([r[...] for r in ye_refs], axis=0)
        acc_ref[...] += jnp.dot(onehot, rows, preferred_element_type=F32)

    @pl.when(last_ref[i] == 1)
    def _():
        y = x_ref[...] + acc_ref[...]
        y_ref[...] = _rms(y, g_ref[...]) if final_norm else y


def _scatter_items(bounds, cap, tb):
    e, nkb1 = bounds.shape
    nkb = nkb1 - 1
    win, grp = SCATTER_WINDOW, SCATTER_WINDOWS
    c0, c1 = bounds[:, :-1].T, bounds[:, 1:].T
    a0 = (c0 // ROW_ALIGN) * ROW_ALIGN
    nwin = jnp.where(c1 > c0, (c1 - a0 + win - 1) // win, 0)
    wmax = (tb + ROW_ALIGN - 1 + win - 1) // win
    w = jnp.arange(wmax, dtype=I32)
    valid = (w[None, None, :] < nwin[:, :, None]).reshape(-1)
    start = (a0[:, :, None] + w * win).reshape(-1)
    base = jnp.minimum(start, cap - win)
    expert = jnp.broadcast_to(jnp.arange(e, dtype=I32)[None, :, None], (nkb, e, wmax)).reshape(-1)
    block = jnp.broadcast_to(jnp.arange(nkb, dtype=I32)[:, None, None], (nkb, e, wmax)).reshape(-1)
    per_block = jnp.sum(valid.reshape(nkb, -1).astype(I32), axis=1)
    items_per_block = jnp.maximum((per_block + grp - 1) // grp, 1)
    item_end = jnp.cumsum(items_per_block)
    item_start = item_end - items_per_block
    rank = jnp.cumsum(valid.astype(I32)) - 1 - (jnp.cumsum(per_block) - per_block)[block]
    max_windows = e * cap // win + (e * nkb * (win + ROW_ALIGN - 2) + win - 1) // win
    max_items = (max_windows + grp - 1) // grp + nkb
    dest = jnp.where(valid, (item_start[block] + rank // grp) * grp + rank % grp, max_items * grp)

    def place(values, fill):
        return jnp.full((max_items * grp,), fill, I32).at[dest].set(values.astype(I32), mode="drop")

    it = jnp.arange(max_items, dtype=I32)
    real = it < item_end[-1]
    kb = jnp.minimum(jnp.sum((it[:, None] >= item_end[None, :]).astype(I32), axis=1), nkb - 1)
    first = real & (it == item_start[kb])
    last = real & (it == item_end[kb] - 1)
    return (kb, first.astype(I32), last.astype(I32), real.astype(I32),
            place(expert, 0), place((expert * cap + base) // ROW_ALIGN, 0), place(base, NO_SLOT), place(start, 0))


def moe_scatter_norm(ye, pos_t, gate_t, x, g_final, final_norm, bounds, cap, tb):
    t, d = x.shape
    e = pos_t.shape[0]
    assert 3 * e <= LANES and cap % SCATTER_WINDOW == 0
    pos_c = pos_t.T
    hi = jnp.floor(pos_c / SLOT_SPLIT)
    tok = jnp.concatenate([hi, pos_c - SLOT_SPLIT * hi, gate_t.T, jnp.zeros((t, LANES - 3 * e), F32)],
                          axis=1).astype(BF16)
    items = _scatter_items(bounds, cap, tb)
    grp = SCATTER_WINDOWS
    tokmap = lambda i, kb, *_: (kb[i], 0)

    def window(k):
        return pl.BlockSpec((pl.Element(SCATTER_WINDOW), pl.Element(d)),
                            lambda i, kb, f, l, r, ex, off, base, vlo: (off[i * grp + k] * ROW_ALIGN, 0))

    grid_spec = pltpu.PrefetchScalarGridSpec(
        num_scalar_prefetch=8,
        grid=(items[0].shape[0],),
        in_specs=[pl.BlockSpec((tb, LANES), tokmap)] + [window(k) for k in range(grp)]
                 + [pl.BlockSpec((tb, d), tokmap), pl.BlockSpec((1, d), lambda i, *_: (0, 0))],
        out_specs=pl.BlockSpec((tb, d), tokmap),
        scratch_shapes=[pltpu.VMEM((tb, d), F32)],
    )
    return pl.pallas_call(
        functools.partial(_scatter_kernel, experts=e, final_norm=final_norm),
        grid_spec=grid_spec,
        out_shape=jax.ShapeDtypeStruct((t, d), F32),
        compiler_params=_cparams("arbitrary"),
        name="moe_scatter_norm",
    )(*items, tok, *([ye] * grp), x, g_final.reshape(1, d).astype(F32))


def expert_choice_moe_norm(x, norm_g, w_router, w_gate, w_up, w_down, g_final, final_norm):
    t, d = x.shape
    e = w_router.shape[1]
    cap = CAPACITY_FACTOR * t // e
    sel_tb = _tile(t, 512)
    hn, probs_t = router(x, norm_g, w_router)
    pos_t, gate_t, cnt = select_topk(probs_t, cap, sel_tb)
    counts = cnt[:, :, 0].T.astype(I32)
    bounds = jnp.concatenate([counts, jnp.full((e, 1), cap, I32)], axis=1)
    ts = _tile(cap, 256)
    nj = cap // ts
    g_tb = _tile(t, 1024)
    g_bounds = bounds[:, ::g_tb // sel_tb]
    g_items = _gather_items(g_bounds, ts, nj, e * (nj + t // g_tb))
    xe = moe_gather(hn, pos_t, g_items, cap, ts, g_tb)
    ye = moe_ffn(xe, w_gate, w_up, w_down, cap)
    return moe_scatter_norm(ye, pos_t, gate_t, x, g_final, final_norm, bounds, cap, sel_tb)


def _prepare(p, depth_i):
    i = depth_i
    ssd_w = p["ssd_norm"].shape[-1]
    xbc_w = p["conv_w"].shape[-1]
    heads = p["ssd_a_log"].shape[-1]
    w_in = p["w_in"][i]
    o_z, o_x, o_dt = ssd_w, ssd_w + xbc_w, ssd_w + xbc_w + heads
    w_zxu = jnp.concatenate([w_in[:, o_z:o_x], w_in[:, :o_z], w_in[:, o_dt:]], axis=1).astype(BF16)
    w_dt = _pad_lanes(w_in[:, o_x:o_dt]).astype(BF16)
    dims = dict(heads=heads, hdim=ssd_w // heads, groups=SSD_GROUPS,
                nstate=(xbc_w - ssd_w) // (2 * SSD_GROUPS))
    tables = _s5_tables(p["s5_a_re"][i], p["s5_a_im"][i], p["s5_log_step"][i], p["s5_b_re"][i],
                        p["s5_b_im"][i], p["s5_c_re"][i], p["s5_c_im"][i])
    w_out = p["w_out"][i].astype(BF16)
    return dict(
        w_zxu=w_zxu, w_dt=w_dt, dims=dims, tables=tables, u_blk=(xbc_w + ssd_w) // (w_in.shape[1] - o_dt),
        w_out_ssd=w_out[:ssd_w], w_out_s5=w_out[ssd_w:],
        w_q=p["w_q"][i].astype(BF16),
        w_kv=jnp.concatenate([p["w_k"][i], p["w_v"][i]], axis=1).astype(BF16),
        w_o=p["w_o"][i].astype(BF16),
        w_gate=p["w_gate"][i], w_up=p["w_up"][i], w_down=p["w_down"][i],
    )


def _trunk(x, mem, p, prepared):
    batch, seq, d = x.shape
    mem_len = mem.shape[1]
    t = batch * seq
    x = x.reshape(t, d)
    mem = mem.reshape(batch * mem_len, d)
    depth = p["w_in"].shape[0]
    for i in range(depth):
        w = prepared[i]
        zxu = norm_matmul(x, p["norm_mix"][i], w["w_zxu"], BF16)
        dt_raw = norm_matmul(x, p["norm_mix"][i], w["w_dt"], F32)
        y_ssd = ssd_mixer(zxu, dt_raw, batch, seq, p["conv_w"][i], p["conv_b"][i], p["ssd_dt_bias"][i],
                          p["ssd_a_log"][i], p["ssd_d"][i], p["ssd_norm"][i], w["dims"])
        y_s5 = s5_mixer(zxu, w["u_blk"], batch, seq, w["tables"], p["s5_d"][i], p["s5_w_glu"][i],
                        p["s5_norm"][i])
        x = matmul_residual([y_ssd, y_s5], [w["w_out_ssd"], w["w_out_s5"]], x)
        q = norm_matmul(x, p["norm_attn"][i], w["w_q"], BF16)
        kv = norm_matmul(mem, p["norm_mem"][i], w["w_kv"], BF16)
        o = mem_attention(q, kv, batch, seq, mem_len)
        x = matmul_residual([o], [w["w_o"]], x)
        x = expert_choice_moe_norm(x, p["norm_ffn"][i], p["w_router"][i], w["w_gate"], w["w_up"],
                                   w["w_down"], p["norm_final"], i == depth - 1)
    return x.reshape(batch, seq, d)


def kernel(x_prompt, x_sample, mem_prompt, mem_sample, norm_mix, w_in, conv_w, conv_b, ssd_dt_bias,
           ssd_a_log, ssd_d, ssd_norm, s5_a_re, s5_a_im, s5_log_step, s5_b_re, s5_b_im, s5_c_re,
           s5_c_im, s5_d, s5_w_glu, s5_norm, w_out, norm_attn, norm_mem, w_q, w_k, w_v, w_o,
           norm_ffn, w_router, w_gate, w_up, w_down, norm_final):
    p = dict(norm_mix=norm_mix, w_in=w_in, conv_w=conv_w, conv_b=conv_b, ssd_dt_bias=ssd_dt_bias,
             ssd_a_log=ssd_a_log, ssd_d=ssd_d, ssd_norm=ssd_norm, s5_a_re=s5_a_re, s5_a_im=s5_a_im,
             s5_log_step=s5_log_step, s5_b_re=s5_b_re, s5_b_im=s5_b_im, s5_c_re=s5_c_re,
             s5_c_im=s5_c_im, s5_d=s5_d, s5_w_glu=s5_w_glu, s5_norm=s5_norm, w_out=w_out,
             norm_attn=norm_attn, norm_mem=norm_mem, w_q=w_q, w_k=w_k, w_v=w_v, w_o=w_o,
             norm_ffn=norm_ffn, w_router=w_router, w_gate=w_gate, w_up=w_up, w_down=w_down,
             norm_final=norm_final)
    prepared = [_prepare(p, i) for i in range(w_in.shape[0])]
    return (_trunk(x_prompt, mem_prompt, p, prepared), _trunk(x_sample, mem_sample, p, prepared))
```

```python
import functools

import jax
import jax.numpy as jnp
from jax import lax
from jax.experimental import pallas as pl
from jax.experimental.pallas import tpu as pltpu

F32 = jnp.float32
BF16 = jnp.bfloat16
I32 = jnp.int32
HIGHEST = lax.Precision.HIGHEST

EPS = 1e-6
SSD_GROUPS = 4
MEM_HEADS = 4
CAPACITY_FACTOR = 2
LANES = 128
HALO_ROWS = 16
SSD_CHUNK = 256
S5_CHUNK = 16
VMEM_LIMIT = 56 * 1024 * 1024


def _cparams(*sem):
    return pltpu.CompilerParams(dimension_semantics=sem, vmem_limit_bytes=VMEM_LIMIT)


def _tile(n, target):
    if n <= target:
        return n
    t = (target // LANES) * LANES
    while n % t:
        t -= LANES
    return t


def _rms(x, g):
    return x * lax.rsqrt(jnp.mean(x * x, axis=-1, keepdims=True) + EPS) * g


def _norm_mm_kernel(x_ref, g_ref, w_ref, o_ref, h_scr):
    @pl.when(pl.program_id(1) == 0)
    def _():
        h_scr[...] = _rms(x_ref[...].astype(F32), g_ref[...]).astype(BF16)

    o_ref[...] = jnp.dot(h_scr[...], w_ref[...], preferred_element_type=F32).astype(o_ref.dtype)


def norm_matmul(x, g, w, out_dtype, tm=1024, tn=1024):
    t, d = x.shape
    n = w.shape[1]
    tm = _tile(t, tm)
    tn = _tile(n, tn)
    return pl.pallas_call(
        _norm_mm_kernel,
        grid=(t // tm, n // tn),
        in_specs=[pl.BlockSpec((tm, d), lambda i, j: (i, 0)),
                  pl.BlockSpec((1, d), lambda i, j: (0, 0)),
                  pl.BlockSpec((d, tn), lambda i, j: (0, j))],
        out_specs=pl.BlockSpec((tm, tn), lambda i, j: (i, j)),
        out_shape=jax.ShapeDtypeStruct((t, n), out_dtype),
        scratch_shapes=[pltpu.VMEM((tm, d), BF16)],
        compiler_params=_cparams("parallel", "arbitrary"),
        name="norm_matmul",
    )(x, g.reshape(1, d).astype(F32), w)


def _mm_res_kernel(*refs, n_a):
    a_refs = refs[:n_a]
    w_refs = refs[n_a:2 * n_a]
    r_ref, o_ref = refs[2 * n_a], refs[2 * n_a + 1]
    acc = r_ref[...].astype(F32)
    for a_ref, w_ref in zip(a_refs, w_refs):
        acc = acc + jnp.dot(a_ref[...], w_ref[...], preferred_element_type=F32)
    o_ref[...] = acc


def matmul_residual(a_list, w_list, r, tm=1024, tn=1024):
    t, n = r.shape
    tm = _tile(t, tm)
    tn = _tile(n, tn)
    n_a = len(a_list)
    in_specs = [pl.BlockSpec((tm, a.shape[1]), lambda i, j: (i, 0)) for a in a_list]
    in_specs += [pl.BlockSpec((w.shape[0], tn), lambda i, j: (0, j)) for w in w_list]
    in_specs += [pl.BlockSpec((tm, tn), lambda i, j: (i, j))]
    return pl.pallas_call(
        functools.partial(_mm_res_kernel, n_a=n_a),
        grid=(t // tm, n // tn),
        in_specs=in_specs,
        out_specs=pl.BlockSpec((tm, tn), lambda i, j: (i, j)),
        out_shape=jax.ShapeDtypeStruct((t, n), F32),
        compiler_params=_cparams("parallel", "arbitrary"),
        name="matmul_residual",
    )(*a_list, *w_list, r)


def _pair_expand(v, h0, p):
    q = v.shape[0]
    lane = lax.broadcasted_iota(I32, (q, 2 * p), 1)
    return jnp.where(lane < p, v[:, h0:h0 + 1], v[:, h0 + 1:h0 + 2])


def _ssd_core(xs, bm, cm, dtr, bias, alog, state_ref, *, reverse, heads, hdim, groups, nstate):
    q = xs.shape[0]
    per_group = heads // groups
    assert per_group % 2 == 0 and 2 * hdim == LANES
    dt = jax.nn.softplus(dtr + bias)
    dta = dt * (-jnp.exp(alog))
    row = lax.broadcasted_iota(I32, (q, q), 0)
    col = lax.broadcasted_iota(I32, (q, q), 1)
    mask = (row <= col) if reverse else (row >= col)
    cs = jnp.dot(mask.astype(F32), dta, precision=HIGHEST, preferred_element_type=F32)
    tot = cs[0:1, :] if reverse else cs[q - 1:q, :]
    cs_t = cs.T
    e_out = jnp.exp(cs)
    e_in = jnp.exp(tot - cs)
    e_tot = jnp.exp(tot)
    lane_lo = lax.broadcasted_iota(I32, (q, 2 * hdim), 1) < hdim
    ys = []
    for g in range(groups):
        b_g = bm[:, g * nstate:(g + 1) * nstate]
        c_g = cm[:, g * nstate:(g + 1) * nstate].astype(BF16)
        cb = lax.dot_general(c_g, b_g.astype(BF16), (((1,), (1,)), ((), ())),
                             preferred_element_type=F32)
        st = state_ref[g]
        y_off = jnp.dot(c_g, st.astype(BF16), preferred_element_type=F32)
        x_in, dec = [], []
        for pr in range(per_group // 2):
            h0 = g * per_group + 2 * pr
            c0 = h0 * hdim
            xdt = xs[:, c0:c0 + 2 * hdim] * _pair_expand(dt, h0, hdim)
            xdt16 = xdt.astype(BF16)
            y_pair = []
            for k in range(2):
                h = h0 + k
                seg = jnp.exp(jnp.where(mask, cs[:, h:h + 1] - cs_t[h:h + 1, :], -jnp.inf))
                m = (cb * seg).astype(BF16)
                y_pair.append(jnp.dot(m, xdt16, preferred_element_type=F32))
            y_d = jnp.where(lane_lo, y_pair[0], y_pair[1])
            o0 = 2 * pr * hdim
            ys.append(y_d + y_off[:, o0:o0 + 2 * hdim] * _pair_expand(e_out, h0, hdim))
            x_in.append((xdt * _pair_expand(e_in, h0, hdim)).astype(BF16))
            dec.append(_pair_expand(e_tot, h0, hdim))
        x_in = jnp.concatenate(x_in, axis=1) if len(x_in) > 1 else x_in[0]
        dec = jnp.concatenate(dec, axis=1) if len(dec) > 1 else dec[0]
        state_ref[g] = st * dec + jnp.dot(b_g.T.astype(BF16), x_in, preferred_element_type=F32)
    return jnp.concatenate(ys, axis=1)


def _ssd_fwd_kernel(xc_ref, xp_ref, xn_ref, dt_ref, cw_ref, cb_ref, bias_ref, alog_ref,
                    yf_ref, xact_ref, state_ref, *, dims):
    c = pl.program_id(1)
    nc = pl.num_programs(1)

    @pl.when(c == 0)
    def _():
        state_ref[...] = jnp.zeros_like(state_ref)

    x = xc_ref[...].astype(F32)
    q = x.shape[0]
    prev_row = jnp.where(c > 0, xp_ref[HALO_ROWS - 1:HALO_ROWS, :].astype(F32), 0.0)
    next_row = jnp.where(c < nc - 1, xn_ref[0:1, :].astype(F32), 0.0)
    rows = lax.broadcasted_iota(I32, (q, 1), 0)
    x_dn = jnp.where(rows == 0, prev_row, pltpu.roll(x, 1, 0))
    x_up = jnp.where(rows == q - 1, next_row, pltpu.roll(x, q - 1, 0))
    xc = cw_ref[0:1, :] * x_dn + cw_ref[1:2, :] * x + cw_ref[2:3, :] * x_up + cb_ref[...]
    xa = xc * jax.nn.sigmoid(xc)
    xact_ref[...] = xa.astype(BF16)
    w = dims["heads"] * dims["hdim"]
    gn = dims["groups"] * dims["nstate"]
    y = _ssd_core(xa[:, :w], xa[:, w:w + gn], xa[:, w + gn:], dt_ref[...], bias_ref[...],
                  alog_ref[...], state_ref, reverse=False, **dims)
    yf_ref[...] = y.astype(yf_ref.dtype)


def _ssd_bwd_kernel(xa_ref, dt_ref, yf_ref, z_ref, bias_ref, alog_ref, dsk_ref, ng_ref,
                    y_ref, state_ref, *, dims):
    @pl.when(pl.program_id(1) == 0)
    def _():
        state_ref[...] = jnp.zeros_like(state_ref)

    xa = xa_ref[...].astype(F32)
    w = dims["heads"] * dims["hdim"]
    gn = dims["groups"] * dims["nstate"]
    xs = xa[:, :w]
    yb = _ssd_core(xs, xa[:, w:w + gn], xa[:, w + gn:], dt_ref[...], bias_ref[...],
                   alog_ref[...], state_ref, reverse=True, **dims)
    y = yf_ref[...].astype(F32) + yb + dsk_ref[...] * xs
    z = z_ref[...].astype(F32)
    y = y * (z * jax.nn.sigmoid(z))
    gw = w // dims["groups"]
    parts = []
    for g in range(dims["groups"]):
        yg = y[:, g * gw:(g + 1) * gw]
        parts.append(yg * lax.rsqrt(jnp.mean(yg * yg, axis=-1, keepdims=True) + EPS))
    y_ref[...] = (jnp.concatenate(parts, axis=1) * ng_ref[...]).astype(y_ref.dtype)


def _pad_lanes(v):
    v = v.astype(F32)
    return jnp.pad(v, [(0, 0)] * (v.ndim - 1) + [(0, LANES - v.shape[-1])])


def ssd_mixer(zxu, dt_raw, batch, seq, conv_w, conv_b, dt_bias, a_log, d_skip, norm_g, dims):
    t = zxu.shape[0]
    w = dims["heads"] * dims["hdim"]
    xbc_w = conv_w.shape[-1]
    assert xbc_w % w == 0
    q = min(SSD_CHUNK, seq)
    nc = seq // q
    hb = q // HALO_ROWS
    nhalo = t // HALO_ROWS
    state_shape = (dims["groups"], dims["nstate"], (dims["heads"] // dims["groups"]) * dims["hdim"])
    bias = _pad_lanes(dt_bias)
    alog = _pad_lanes(a_log)
    full = lambda n: pl.BlockSpec((1, n), lambda b, c: (0, 0))
    vec = full(LANES)

    yf, xact = pl.pallas_call(
        functools.partial(_ssd_fwd_kernel, dims=dims),
        grid=(batch, nc),
        in_specs=[pl.BlockSpec((q, xbc_w), lambda b, c: (b * nc + c, 0)),
                  pl.BlockSpec((HALO_ROWS, xbc_w), lambda b, c: (jnp.maximum((b * nc + c) * hb - 1, 0), 0)),
                  pl.BlockSpec((HALO_ROWS, xbc_w), lambda b, c: (jnp.minimum((b * nc + c + 1) * hb, nhalo - 1), 0)),
                  pl.BlockSpec((q, LANES), lambda b, c: (b * nc + c, 0)),
                  pl.BlockSpec((3, xbc_w), lambda b, c: (0, 0)),
                  full(xbc_w), vec, vec],
        out_specs=[pl.BlockSpec((q, w), lambda b, c: (b * nc + c, 0)),
                   pl.BlockSpec((q, xbc_w), lambda b, c: (b * nc + c, 0))],
        out_shape=[jax.ShapeDtypeStruct((t, w), BF16), jax.ShapeDtypeStruct((t, xbc_w), BF16)],
        scratch_shapes=[pltpu.VMEM(state_shape, F32)],
        compiler_params=_cparams("parallel", "arbitrary"),
        name="ssd_forward",
    )(zxu, zxu, zxu, dt_raw, conv_w.astype(F32), conv_b.reshape(1, xbc_w).astype(F32), bias[0:1], alog[0:1])

    rev = lambda b, c: (b * nc + nc - 1 - c, 0)
    z_blk = xbc_w // w
    return pl.pallas_call(
        functools.partial(_ssd_bwd_kernel, dims=dims),
        grid=(batch, nc),
        in_specs=[pl.BlockSpec((q, xbc_w), rev),
                  pl.BlockSpec((q, LANES), rev),
                  pl.BlockSpec((q, w), rev),
                  pl.BlockSpec((q, w), lambda b, c: (b * nc + nc - 1 - c, z_blk)),
                  vec, vec, full(w), full(w)],
        out_specs=pl.BlockSpec((q, w), rev),
        out_shape=jax.ShapeDtypeStruct((t, w), BF16),
        scratch_shapes=[pltpu.VMEM(state_shape, F32)],
        compiler_params=_cparams("parallel", "arbitrary"),
        name="ssd_backward",
    )(xact, dt_raw, yf, zxu, bias[1:2], alog[1:2],
      jnp.repeat(d_skip.astype(F32), dims["hdim"]).reshape(1, w), norm_g.reshape(1, w).astype(F32))


def _s5_tables(a_re, a_im, log_step, b_re, b_im, c_re, c_im):
    qs = S5_CHUNK
    a_re, a_im, b_re, b_im, c_re, c_im = (v.astype(F32) for v in (a_re, a_im, b_re, b_im, c_re, c_im))
    _, g, p = a_re.shape
    cg = b_re.shape[-1]
    delta = jnp.exp(log_step.astype(F32))[..., None]
    mag = jnp.exp(a_re * delta)
    ar = mag * jnp.cos(a_im * delta)
    ai = mag * jnp.sin(a_im * delta)
    den = a_re * a_re + a_im * a_im
    qr = ((ar - 1.0) * a_re + ai * a_im) / den
    qi = (ai * a_re - (ar - 1.0) * a_im) / den
    bbr = qr[..., None] * b_re - qi[..., None] * b_im
    bbi = qr[..., None] * b_im + qi[..., None] * b_re
    pr, pi = [jnp.ones_like(ar)], [jnp.zeros_like(ar)]
    for _ in range(qs):
        pr.append(pr[-1] * ar - pi[-1] * ai)
        pi.append(pr[-2] * ai + pi[-1] * ar)
    pr = jnp.stack(pr)
    pi = jnp.stack(pi)
    cpr = c_re[None] * pr[:, :, :, None, :] - c_im[None] * pi[:, :, :, None, :]
    cpi = c_re[None] * pi[:, :, :, None, :] + c_im[None] * pr[:, :, :, None, :]
    wr = pr[..., None] * bbr[None] - pi[..., None] * bbi[None]
    wi = pr[..., None] * bbi[None] + pi[..., None] * bbr[None]
    kern = (jnp.einsum("tdgcp,dgpe->tdgce", cpr[:qs], bbr, precision=HIGHEST)
            - jnp.einsum("tdgcp,dgpe->tdgce", cpi[:qs], bbi, precision=HIGHEST))
    ii = jnp.arange(qs)[:, None]
    jj = jnp.arange(qs)[None, :]
    lag = ii - jj
    t_f = jnp.where((lag >= 0)[:, :, None, None, None], kern[jnp.clip(lag, 0, qs - 1), 0], 0.0)
    t_b = jnp.where((lag <= 0)[:, :, None, None, None], kern[jnp.clip(-lag, 0, qs - 1), 1], 0.0)
    toep = (t_f + t_b).transpose(2, 1, 4, 0, 3).reshape(g, qs * cg, qs * cg)
    def c_rows(d, powers):
        re = cpr[powers, d].transpose(1, 3, 0, 2)
        im = -cpi[powers, d].transpose(1, 3, 0, 2)
        return jnp.concatenate([re, im], axis=1).reshape(g, 2 * p, qs * cg)
    w_out = jnp.concatenate([toep, c_rows(0, jnp.arange(1, qs + 1)), c_rows(1, qs - jnp.arange(qs))], axis=1)
    def b_cols(d, powers):
        re = wr[powers, d].transpose(1, 0, 3, 2)
        im = wi[powers, d].transpose(1, 0, 3, 2)
        return jnp.concatenate([re, im], axis=3).reshape(g, qs * cg, 2 * p)
    w_state = jnp.concatenate([b_cols(0, qs - 1 - jnp.arange(qs)), b_cols(1, jnp.arange(qs))], axis=2)

    def multipliers(d):
        mul = jnp.concatenate([pr[qs, d], pr[qs, d]], axis=-1).reshape(1, g * 2 * p)
        swp = jnp.concatenate([-pi[qs, d], pi[qs, d]], axis=-1).reshape(1, g * 2 * p)
        return mul, swp
    return w_state.astype(BF16), w_out.astype(BF16), multipliers(0) + multipliers(1)


S5_GROUP_BLOCK = 8


def _s5_state_kernel(u_ref, w_ref, vf_ref, vb_ref):
    half = w_ref.shape[2] // 2
    for k in range(u_ref.shape[0]):
        v = jnp.dot(u_ref[k], w_ref[k], preferred_element_type=F32)
        vf_ref[:, k * half:(k + 1) * half] = v[:, :half]
        vb_ref[:, k * half:(k + 1) * half] = v[:, half:]


def _s5_scan_kernel(v_ref, mul_ref, swp_ref, o_ref, s_scr, *, rows, steps, reverse):
    @pl.when(pl.program_id(1) == 0)
    def _():
        s_scr[...] = jnp.zeros_like(s_scr)

    s = s_scr[...]
    mul = mul_ref[...]
    swp = swp_ref[...]
    width = s.shape[1]
    re_lane = (lax.broadcasted_iota(I32, (1, width), 1) % LANES) < (LANES // 2)
    order = range(steps - 1, -1, -1) if reverse else range(steps)
    for k in order:
        o_ref[k * rows:(k + 1) * rows, :] = s
        partner = jnp.where(re_lane, pltpu.roll(s, width - LANES // 2, 1), pltpu.roll(s, LANES // 2, 1))
        s = mul * s + swp * partner + v_ref[k * rows:(k + 1) * rows, :]
    s_scr[...] = s


def _s5_out_kernel(u_ref, sf_ref, sb_ref, w_ref, y_ref):
    qc = u_ref.shape[2]
    ns = (w_ref.shape[1] - qc) // 2
    for k in range(u_ref.shape[0]):
        sf = sf_ref[:, k * ns:(k + 1) * ns].astype(BF16)
        sb = sb_ref[:, k * ns:(k + 1) * ns].astype(BF16)
        y = jnp.dot(u_ref[k], w_ref[k, :qc, :], preferred_element_type=F32)
        y = y + jnp.dot(sf, w_ref[k, qc:qc + ns, :], preferred_element_type=F32)
        y = y + jnp.dot(sb, w_ref[k, qc + ns:, :], preferred_element_type=F32)
        y_ref[k] = y.astype(y_ref.dtype)


def _s5_post_kernel(y_ref, u_ref, d_ref, w_ref, g_ref, o_ref):
    y = y_ref[...].astype(F32) + d_ref[...] * u_ref[...].astype(F32)
    y = jax.nn.gelu(y)
    gate = jax.nn.sigmoid(jnp.dot(y.astype(BF16), w_ref[...], preferred_element_type=F32))
    o_ref[...] = _rms(y * gate, g_ref[...]).astype(o_ref.dtype)


def _group_lane_blocks(n, cg):
    return lax.broadcasted_iota(I32, (n, LANES), 1) // cg


def _s5_split_kernel(u_ref, o_ref, *, groups, cg):
    n = u_ref.shape[0]
    per_vreg = LANES // cg
    width = groups * cg
    blk = _group_lane_blocks(n, cg)
    for v in range(groups // per_vreg):
        rolled = {}
        for j in range(S5_CHUNK):
            src = u_ref[:, j * width + v * LANES:j * width + (v + 1) * LANES].astype(F32)
            for k in range(per_vreg):
                rolled[j, k] = src if k == 0 else pltpu.roll(src, k * cg, 1)
        for gl in range(per_vreg):
            for h in range(S5_CHUNK // per_vreg):
                acc = rolled[h * per_vreg, (-gl) % per_vreg]
                for jj in range(1, per_vreg):
                    acc = jnp.where(blk == jj, rolled[h * per_vreg + jj, (jj - gl) % per_vreg], acc)
                o_ref[v * per_vreg + gl, :, h * LANES:(h + 1) * LANES] = acc.astype(o_ref.dtype)


def _s5_merge_kernel(y_ref, o_ref, *, groups, cg):
    n = y_ref.shape[1]
    per_vreg = LANES // cg
    width = groups * cg
    blk = _group_lane_blocks(n, cg)
    for v in range(groups // per_vreg):
        rolled = {}
        for gl in range(per_vreg):
            for h in range(S5_CHUNK // per_vreg):
                src = y_ref[v * per_vreg + gl, :, h * LANES:(h + 1) * LANES].astype(F32)
                for k in range(per_vreg):
                    rolled[gl, h, k] = src if k == 0 else pltpu.roll(src, k * cg, 1)
        for i in range(S5_CHUNK):
            h, ii = divmod(i, per_vreg)
            acc = rolled[0, h, (-ii) % per_vreg]
            for gl in range(1, per_vreg):
                acc = jnp.where(blk == gl, rolled[gl, h, (gl - ii) % per_vreg], acc)
            o_ref[:, i * width + v * LANES:i * width + (v + 1) * LANES] = acc.astype(o_ref.dtype)


def s5_mixer(zxu, u_blk, batch, seq, tables, d_skip, w_glu, norm_g):
    w_state, w_out, (mul_f, swp_f, mul_b, swp_b) = tables
    t = zxu.shape[0]
    g, qc, ns2 = w_state.shape
    ns = ns2 // 2
    cg = qc // S5_CHUNK
    width = g * cg
    nchunk = seq // S5_CHUNK
    rows = nchunk * batch
    u = zxu[:, u_blk * width:(u_blk + 1) * width]
    u_rows = u.reshape(batch, nchunk, S5_CHUNK * width).transpose(1, 0, 2).reshape(rows, S5_CHUNK * width)
    rn = min(64, rows)
    ug = pl.pallas_call(
        functools.partial(_s5_split_kernel, groups=g, cg=cg),
        grid=(rows // rn,),
        in_specs=[pl.BlockSpec((rn, S5_CHUNK * width), lambda r: (r, 0))],
        out_specs=pl.BlockSpec((g, rn, qc), lambda r: (0, r, 0)),
        out_shape=jax.ShapeDtypeStruct((g, rows, qc), BF16),
        compiler_params=_cparams("parallel"),
        name="s5_split",
    )(u_rows)
    tr = _tile(rows, 1024)
    gb = min(S5_GROUP_BLOCK, g)
    vf, vb = pl.pallas_call(
        _s5_state_kernel,
        grid=(g // gb, rows // tr),
        in_specs=[pl.BlockSpec((gb, tr, qc), lambda i, r: (i, r, 0)),
                  pl.BlockSpec((gb, qc, ns2), lambda i, r: (i, 0, 0))],
        out_specs=[pl.BlockSpec((tr, gb * ns), lambda i, r: (r, i))] * 2,
        out_shape=[jax.ShapeDtypeStruct((rows, g * ns), F32)] * 2,
        compiler_params=_cparams("parallel", "parallel"),
        name="s5_state_in",
    )(ug, w_state)

    panel = _tile(g * ns, 4096)
    steps = min(32, nchunk)
    nblk = nchunk // steps

    def scan(v, mul, swp, reverse):
        rmap = (lambda pnl, i: (nblk - 1 - i, pnl)) if reverse else (lambda pnl, i: (i, pnl))
        return pl.pallas_call(
            functools.partial(_s5_scan_kernel, rows=batch, steps=steps, reverse=reverse),
            grid=(g * ns // panel, nblk),
            in_specs=[pl.BlockSpec((steps * batch, panel), rmap),
                      pl.BlockSpec((1, panel), lambda pnl, i: (0, pnl)),
                      pl.BlockSpec((1, panel), lambda pnl, i: (0, pnl))],
            out_specs=pl.BlockSpec((steps * batch, panel), rmap),
            out_shape=jax.ShapeDtypeStruct(v.shape, F32),
            scratch_shapes=[pltpu.VMEM((batch, panel), F32)],
            compiler_params=_cparams("parallel", "arbitrary"),
            name="s5_scan_bwd" if reverse else "s5_scan_fwd",
        )(v, mul, swp)

    sf = scan(vf, mul_f, swp_f, False)
    sb = scan(vb, mul_b, swp_b, True)
    yg = pl.pallas_call(
        _s5_out_kernel,
        grid=(g // gb, rows // tr),
        in_specs=[pl.BlockSpec((gb, tr, qc), lambda i, r: (i, r, 0)),
                  pl.BlockSpec((tr, gb * ns), lambda i, r: (r, i)),
                  pl.BlockSpec((tr, gb * ns), lambda i, r: (r, i)),
                  pl.BlockSpec((gb, qc + 2 * ns, qc), lambda i, r: (i, 0, 0))],
        out_specs=pl.BlockSpec((gb, tr, qc), lambda i, r: (i, r, 0)),
        out_shape=jax.ShapeDtypeStruct((g, rows, qc), BF16),
        compiler_params=_cparams("parallel", "parallel"),
        name="s5_state_out",
    )(ug, sf, sb, w_out)
    y_rows = pl.pallas_call(
        functools.partial(_s5_merge_kernel, groups=g, cg=cg),
        grid=(rows // rn,),
        in_specs=[pl.BlockSpec((g, rn, qc), lambda r: (0, r, 0))],
        out_specs=pl.BlockSpec((rn, S5_CHUNK * width), lambda r: (r, 0)),
        out_shape=jax.ShapeDtypeStruct((rows, S5_CHUNK * width), BF16),
        compiler_params=_cparams("parallel"),
        name="s5_merge",
    )(yg)
    y = y_rows.reshape(nchunk, batch, S5_CHUNK * width).transpose(1, 0, 2).reshape(t, width)
    tm = _tile(t, 1024)
    return pl.pallas_call(
        _s5_post_kernel,
        grid=(t // tm,),
        in_specs=[pl.BlockSpec((tm, width), lambda i: (i, 0)),
                  pl.BlockSpec((tm, width), lambda i: (i, u_blk)),
                  pl.BlockSpec((1, width), lambda i: (0, 0)),
                  pl.BlockSpec((width, width), lambda i: (0, 0)),
                  pl.BlockSpec((1, width), lambda i: (0, 0))],
        out_specs=pl.BlockSpec((tm, width), lambda i: (i, 0)),
        out_shape=jax.ShapeDtypeStruct((t, width), BF16),
        compiler_params=_cparams("parallel"),
        name="s5_post",
    )(y, zxu, d_skip.reshape(1, width).astype(F32), w_glu.astype(BF16), norm_g.reshape(1, width).astype(F32))


def _attn_kernel(q_ref, kv_ref, o_ref, *, heads):
    d = q_ref.shape[1]
    hd = d // heads
    scale = hd ** -0.5
    for h in range(heads):
        qh = q_ref[:, h * hd:(h + 1) * hd]
        kh = kv_ref[:, h * hd:(h + 1) * hd]
        vh = kv_ref[:, d + h * hd:d + (h + 1) * hd]
        s = lax.dot_general(qh, kh, (((1,), (1,)), ((), ())), preferred_element_type=F32) * scale
        p = jnp.exp(s - jnp.max(s, axis=-1, keepdims=True))
        o = jnp.dot(p.astype(BF16), vh, preferred_element_type=F32) / jnp.sum(p, axis=-1, keepdims=True)
        o_ref[:, h * hd:(h + 1) * hd] = o.astype(o_ref.dtype)


def mem_attention(q, kv, batch, seq, mem_len):
    t, d = q.shape
    tq = _tile(seq, 1024)
    nq = seq // tq
    return pl.pallas_call(
        functools.partial(_attn_kernel, heads=MEM_HEADS),
        grid=(batch, nq),
        in_specs=[pl.BlockSpec((tq, d), lambda b, i: (b * nq + i, 0)),
                  pl.BlockSpec((mem_len, 2 * d), lambda b, i: (b, 0))],
        out_specs=pl.BlockSpec((tq, d), lambda b, i: (b * nq + i, 0)),
        out_shape=jax.ShapeDtypeStruct((t, d), BF16),
        compiler_params=_cparams("parallel", "arbitrary"),
        name="mem_attention",
    )(q, kv)


def _router_kernel(x_ref, g_ref, wr_ref, hn_ref, p_ref):
    hn = _rms(x_ref[...], g_ref[...])
    hn_hi = hn.astype(BF16)
    hn_ref[...] = hn_hi
    hn_lo = (hn - hn_hi.astype(F32)).astype(BF16)
    w = wr_ref[...]
    w_hi = w.astype(BF16)
    w_lo = (w - w_hi.astype(F32)).astype(BF16)
    nt = lambda a, b: lax.dot_general(a, b, (((1,), (1,)), ((), ())), preferred_element_type=F32)
    logits = nt(w_hi, hn_hi) + nt(w_lo, hn_hi) + nt(w_hi, hn_lo)
    e = jnp.exp(logits - jnp.max(logits, axis=0, keepdims=True))
    p_ref[...] = e / jnp.sum(e, axis=0, keepdims=True)


def router(x, g, w_router, tm=1024):
    t, d = x.shape
    e = w_router.shape[1]
    tm = _tile(t, tm)
    return pl.pallas_call(
        _router_kernel,
        grid=(t // tm,),
        in_specs=[pl.BlockSpec((tm, d), lambda i: (i, 0)),
                  pl.BlockSpec((1, d), lambda i: (0, 0)),
                  pl.BlockSpec((e, d), lambda i: (0, 0))],
        out_specs=[pl.BlockSpec((tm, d), lambda i: (i, 0)),
                   pl.BlockSpec((e, tm), lambda i: (0, i))],
        out_shape=[jax.ShapeDtypeStruct((t, d), BF16), jax.ShapeDtypeStruct((e, t), F32)],
        compiler_params=_cparams("parallel"),
        name="router",
    )(x, g.reshape(1, d).astype(F32), w_router.T.astype(F32))


def _select_kernel(p_ref, pos_ref, gate_ref, cnt_ref, *, cap, tb):
    e, t = p_ref.shape
    bits = pltpu.bitcast(p_ref[...], I32)

    def search(i, prefix):
        cand = prefix | jnp.left_shift(jnp.int32(1), 30 - i)
        cnt = jnp.sum((bits >= cand).astype(I32), axis=1, keepdims=True)
        return jnp.where(cnt >= cap, cand, prefix)

    thr = lax.fori_loop(0, 31, search, jnp.zeros((e, 1), I32))
    need = (cap - jnp.sum((bits > thr).astype(I32), axis=1, keepdims=True)).astype(F32)
    r = lax.broadcasted_iota(I32, (tb, tb), 0)
    c = lax.broadcasted_iota(I32, (tb, tb), 1)
    incl = (r <= c).astype(BF16)
    strict = (r < c).astype(BF16)

    def block(k, carry):
        carry_eq, carry_sel = carry
        sl = pl.ds(pl.multiple_of(k * tb, tb), tb)
        p = p_ref[:, sl]
        b = pltpu.bitcast(p, I32)
        eq = b == thr
        eq16 = jnp.where(eq, 1.0, 0.0).astype(BF16)
        rank = jnp.dot(eq16, incl, preferred_element_type=F32) + carry_eq
        sel = (b > thr) | (eq & (rank <= need))
        sel16 = jnp.where(sel, 1.0, 0.0).astype(BF16)
        pos = jnp.dot(sel16, strict, preferred_element_type=F32) + carry_sel
        pos_ref[:, sl] = jnp.where(sel, pos, -1.0)
        gate_ref[:, sl] = jnp.where(sel, p, 0.0)
        cnt_ref[k] = jnp.broadcast_to(carry_sel, (e, LANES))
        return (carry_eq + jnp.sum(eq16.astype(F32), axis=1, keepdims=True),
                carry_sel + jnp.sum(sel16.astype(F32), axis=1, keepdims=True))

    lax.fori_loop(0, t // tb, block, (jnp.zeros((e, 1), F32), jnp.zeros((e, 1), F32)))


def select_topk(probs_t, cap, tb):
    e, t = probs_t.shape
    nb = t // tb
    return pl.pallas_call(
        functools.partial(_select_kernel, cap=cap, tb=tb),
        grid=(1,),
        in_specs=[pl.BlockSpec((e, t), lambda i: (0, 0))],
        out_specs=[pl.BlockSpec((e, t), lambda i: (0, 0)),
                   pl.BlockSpec((e, t), lambda i: (0, 0)),
                   pl.BlockSpec((nb, e, LANES), lambda i: (0, 0, 0))],
        out_shape=[jax.ShapeDtypeStruct((e, t), F32), jax.ShapeDtypeStruct((e, t), F32),
                   jax.ShapeDtypeStruct((nb, e, LANES), F32)],
        compiler_params=_cparams("arbitrary"),
        name="select_topk",
    )(probs_t)


GATHER_WINDOW = 256


def _gather_kernel(a0_ref, nw_ref, pos_ref, hn_ref, xe_ref, *, cap, win):
    e = pl.program_id(0)
    kb = pl.program_id(1)

    @pl.when(kb == 0)
    def _():
        xe_ref[...] = jnp.zeros_like(xe_ref)

    item = e * pl.num_programs(1) + kb
    tb = hn_ref.shape[0]
    slot = pos_ref[pl.ds(e, 1), :]
    local = lax.broadcasted_iota(I32, (win, tb), 0).astype(F32)

    def window(w, carry):
        start = a0_ref[item] * ROW_ALIGN + w * win
        base = jnp.minimum(a0_ref[item] + w * (win // ROW_ALIGN), (cap - win) // ROW_ALIGN) * ROW_ALIGN
        hit = (slot - base.astype(F32) == local) & (slot >= start.astype(F32))
        rows = jnp.dot(jnp.where(hit, 1.0, 0.0).astype(BF16), hn_ref[...], preferred_element_type=F32)
        sl = pl.ds(pl.multiple_of(base, ROW_ALIGN), win)
        xe_ref[sl, :] = xe_ref[sl, :] + rows.astype(xe_ref.dtype)
        return carry

    lax.fori_loop(0, nw_ref[item], window, 0)


def moe_gather(hn, pos_t, bounds, cap, tb):
    t, d = hn.shape
    e = pos_t.shape[0]
    win = min(GATHER_WINDOW, cap)
    assert cap % win == 0 and win % ROW_ALIGN == 0
    c0, c1 = bounds[:, :-1], bounds[:, 1:]
    a0 = c0 // ROW_ALIGN
    nw = jnp.where(c1 > c0, (c1 - a0 * ROW_ALIGN + win - 1) // win, 0)
    grid_spec = pltpu.PrefetchScalarGridSpec(
        num_scalar_prefetch=2,
        grid=(e, t // tb),
        in_specs=[pl.BlockSpec((e, tb), lambda x, k, a0, nw: (0, k)),
                  pl.BlockSpec((tb, d), lambda x, k, a0, nw: (k, 0))],
        out_specs=pl.BlockSpec((cap, d), lambda x, k, a0, nw: (x, 0)),
    )
    return pl.pallas_call(
        functools.partial(_gather_kernel, cap=cap, win=win),
        grid_spec=grid_spec,
        out_shape=jax.ShapeDtypeStruct((e * cap, d), BF16),
        compiler_params=_cparams("parallel", "arbitrary"),
        name="moe_gather",
    )(a0.reshape(-1).astype(I32), nw.reshape(-1).astype(I32), pos_t, hn)


def _ffn_kernel(xe_ref, wg_ref, wu_ref, wd_ref, ye_ref, acc_ref):
    f = pl.program_id(2)

    @pl.when(f == 0)
    def _():
        acc_ref[...] = jnp.zeros_like(acc_ref)

    xe = xe_ref[...]
    gate = jnp.dot(xe, wg_ref[0].astype(BF16), preferred_element_type=F32)
    up = jnp.dot(xe, wu_ref[0].astype(BF16), preferred_element_type=F32)
    he = (gate * jax.nn.sigmoid(gate) * up).astype(BF16)
    acc_ref[...] += jnp.dot(he, wd_ref[0].astype(BF16), preferred_element_type=F32)

    @pl.when(f == pl.num_programs(2) - 1)
    def _():
        ye_ref[...] = acc_ref[...].astype(ye_ref.dtype)


def moe_ffn(xe, w_gate, w_up, w_down, cap, ts=1024, tf=256):
    n, d = xe.shape
    e, _, ff = w_gate.shape
    ts = _tile(cap, ts)
    tf = _tile(ff, tf)
    nsb = cap // ts
    return pl.pallas_call(
        _ffn_kernel,
        grid=(e, nsb, ff // tf),
        in_specs=[pl.BlockSpec((ts, d), lambda x, s, f: (x * nsb + s, 0)),
                  pl.BlockSpec((1, d, tf), lambda x, s, f: (x, 0, f)),
                  pl.BlockSpec((1, d, tf), lambda x, s, f: (x, 0, f)),
                  pl.BlockSpec((1, tf, d), lambda x, s, f: (x, f, 0))],
        out_specs=pl.BlockSpec((ts, d), lambda x, s, f: (x * nsb + s, 0)),
        out_shape=jax.ShapeDtypeStruct((n, d), BF16),
        scratch_shapes=[pltpu.VMEM((ts, d), F32)],
        compiler_params=_cparams("parallel", "parallel", "arbitrary"),
        name="moe_ffn",
    )(xe, w_gate, w_up, w_down)


SLOT_SPLIT = 64
SCATTER_WINDOW = 128
SCATTER_WINDOWS = 8
ROW_ALIGN = 16
NO_SLOT = 1 << 24


def _scatter_kernel(kb_ref, first_ref, last_ref, real_ref, ex_ref, off_ref, base_ref, vlo_ref,
                    tok_ref, *rest, experts, final_norm):
    ye_refs = rest[:SCATTER_WINDOWS]
    x_ref, g_ref, y_ref, acc_ref = rest[SCATTER_WINDOWS:]
    i = pl.program_id(0)
    win = SCATTER_WINDOW

    @pl.when(first_ref[i] == 1)
    def _():
        acc_ref[...] = jnp.zeros_like(acc_ref)

    @pl.when(real_ref[i] == 1)
    def _():
        tok = tok_ref[...]
        kdim = tok.shape[1]
        e = experts
        row = lax.broadcasted_iota(I32, (kdim, 2 * win), 0)
        col = lax.broadcasted_iota(I32, (kdim, 2 * win), 1)
        lane = lax.broadcasted_iota(I32, (1, win), 1).astype(F32)
        pieces = []
        for k in range(SCATTER_WINDOWS):
            w = i * SCATTER_WINDOWS + k
            ek = ex_ref[w]
            sel = jnp.where((row == ek) & (col < win), float(SLOT_SPLIT), 0.0)
            sel = sel + jnp.where((row == e + ek) & (col < win), 1.0, 0.0)
            sel = sel + jnp.where((row == 2 * e + ek) & (col >= win), 1.0, 0.0)
            both = jnp.dot(tok, sel.astype(BF16), preferred_element_type=F32)
            slot = both[:, :win]
            hit = (slot - base_ref[w].astype(F32) == lane) & (slot >= vlo_ref[w].astype(F32))
            pieces.append(jnp.where(hit, both[:, win:], 0.0).astype(BF16))
        onehot = jnp.concatenate(pieces, axis=1)
        rows = jnp.concatenate([r[...] for r in ye_refs], axis=0)
        acc_ref[...] += jnp.dot(onehot, rows, preferred_element_type=F32)

    @pl.when(last_ref[i] == 1)
    def _():
        y = x_ref[...] + acc_ref[...]
        y_ref[...] = _rms(y, g_ref[...]) if final_norm else y


def _scatter_items(bounds, cap, tb):
    e, nkb1 = bounds.shape
    nkb = nkb1 - 1
    win, grp = SCATTER_WINDOW, SCATTER_WINDOWS
    c0, c1 = bounds[:, :-1].T, bounds[:, 1:].T
    a0 = (c0 // ROW_ALIGN) * ROW_ALIGN
    nwin = jnp.where(c1 > c0, (c1 - a0 + win - 1) // win, 0)
    wmax = (tb + ROW_ALIGN - 1 + win - 1) // win
    w = jnp.arange(wmax, dtype=I32)
    valid = (w[None, None, :] < nwin[:, :, None]).reshape(-1)
    start = (a0[:, :, None] + w * win).reshape(-1)
    base = jnp.minimum(start, cap - win)
    expert = jnp.broadcast_to(jnp.arange(e, dtype=I32)[None, :, None], (nkb, e, wmax)).reshape(-1)
    block = jnp.broadcast_to(jnp.arange(nkb, dtype=I32)[:, None, None], (nkb, e, wmax)).reshape(-1)
    per_block = jnp.sum(valid.reshape(nkb, -1).astype(I32), axis=1)
    items_per_block = jnp.maximum((per_block + grp - 1) // grp, 1)
    item_end = jnp.cumsum(items_per_block)
    item_start = item_end - items_per_block
    rank = jnp.cumsum(valid.astype(I32)) - 1 - (jnp.cumsum(per_block) - per_block)[block]
    max_windows = e * cap // win + (e * nkb * (win + ROW_ALIGN - 2) + win - 1) // win
    max_items = (max_windows + grp - 1) // grp + nkb
    dest = jnp.where(valid, (item_start[block] + rank // grp) * grp + rank % grp, max_items * grp)

    fill = jnp.broadcast_to(jnp.array([0, 0, NO_SLOT, 0], I32), (max_items * grp, 4))
    values = jnp.stack([expert, (expert * cap + base) // ROW_ALIGN, base, start], axis=1).astype(I32)
    placed = fill.at[dest].set(values, mode="drop")
    it = jnp.arange(max_items, dtype=I32)
    real = it < item_end[-1]
    kb = jnp.minimum(jnp.sum((it[:, None] >= item_end[None, :]).astype(I32), axis=1), nkb - 1)
    first = real & (it == item_start[kb])
    last = real & (it == item_end[kb] - 1)
    return (kb, first.astype(I32), last.astype(I32), real.astype(I32),
            placed[:, 0], placed[:, 1], placed[:, 2], placed[:, 3])


def moe_scatter_norm(ye, pos_t, gate_t, x, g_final, final_norm, bounds, cap, tb):
    t, d = x.shape
    e = pos_t.shape[0]
    assert 3 * e <= LANES and cap % SCATTER_WINDOW == 0
    pos_c = pos_t.T
    hi = jnp.floor(pos_c / SLOT_SPLIT)
    tok = jnp.concatenate([hi, pos_c - SLOT_SPLIT * hi, gate_t.T, jnp.zeros((t, LANES - 3 * e), F32)],
                          axis=1).astype(BF16)
    items = _scatter_items(bounds, cap, tb)
    grp = SCATTER_WINDOWS
    tokmap = lambda i, kb, *_: (kb[i], 0)

    def window(k):
        return pl.BlockSpec((pl.Element(SCATTER_WINDOW), pl.Element(d)),
                            lambda i, kb, f, l, r, ex, off, base, vlo: (off[i * grp + k] * ROW_ALIGN, 0))

    grid_spec = pltpu.PrefetchScalarGridSpec(
        num_scalar_prefetch=8,
        grid=(items[0].shape[0],),
        in_specs=[pl.BlockSpec((tb, LANES), tokmap)] + [window(k) for k in range(grp)]
                 + [pl.BlockSpec((tb, d), tokmap), pl.BlockSpec((1, d), lambda i, *_: (0, 0))],
        out_specs=pl.BlockSpec((tb, d), tokmap),
        scratch_shapes=[pltpu.VMEM((tb, d), F32)],
    )
    return pl.pallas_call(
        functools.partial(_scatter_kernel, experts=e, final_norm=final_norm),
        grid_spec=grid_spec,
        out_shape=jax.ShapeDtypeStruct((t, d), F32),
        compiler_params=_cparams("arbitrary"),
        name="moe_scatter_norm",
    )(*items, tok, *([ye] * grp), x, g_final.reshape(1, d).astype(F32))


def expert_choice_moe_norm(x, norm_g, w_router, w_gate, w_up, w_down, g_final, final_norm):
    t, d = x.shape
    e = w_router.shape[1]
    cap = CAPACITY_FACTOR * t // e
    sel_tb = _tile(t, 512)
    hn, probs_t = router(x, norm_g, w_router)
    pos_t, gate_t, cnt = select_topk(probs_t, cap, sel_tb)
    counts = cnt[:, :, 0].T.astype(I32)
    bounds = jnp.concatenate([counts, jnp.full((e, 1), cap, I32)], axis=1)
    g_tb = _tile(t, 1024)
    xe = moe_gather(hn, pos_t, bounds[:, ::g_tb // sel_tb], cap, g_tb)
    ye = moe_ffn(xe, w_gate, w_up, w_down, cap)
    return moe_scatter_norm(ye, pos_t, gate_t, x, g_final, final_norm, bounds, cap, sel_tb)


def _prepare(p, depth_i):
    i = depth_i
    ssd_w = p["ssd_norm"].shape[-1]
    xbc_w = p["conv_w"].shape[-1]
    heads = p["ssd_a_log"].shape[-1]
    w_in = p["w_in"][i]
    o_z, o_x, o_dt = ssd_w, ssd_w + xbc_w, ssd_w + xbc_w + heads
    w_zxu = jnp.concatenate([w_in[:, o_z:o_x], w_in[:, :o_z], w_in[:, o_dt:]], axis=1).astype(BF16)
    w_dt = _pad_lanes(w_in[:, o_x:o_dt]).astype(BF16)
    dims = dict(heads=heads, hdim=ssd_w // heads, groups=SSD_GROUPS,
                nstate=(xbc_w - ssd_w) // (2 * SSD_GROUPS))
    tables = _s5_tables(p["s5_a_re"][i], p["s5_a_im"][i], p["s5_log_step"][i], p["s5_b_re"][i],
                        p["s5_b_im"][i], p["s5_c_re"][i], p["s5_c_im"][i])
    w_out = p["w_out"][i].astype(BF16)
    return dict(
        w_zxu=w_zxu, w_dt=w_dt, dims=dims, tables=tables, u_blk=(xbc_w + ssd_w) // (w_in.shape[1] - o_dt),
        w_out_ssd=w_out[:ssd_w], w_out_s5=w_out[ssd_w:],
        w_q=p["w_q"][i].astype(BF16),
        w_kv=jnp.concatenate([p["w_k"][i], p["w_v"][i]], axis=1).astype(BF16),
        w_o=p["w_o"][i].astype(BF16),
        w_gate=p["w_gate"][i], w_up=p["w_up"][i], w_down=p["w_down"][i],
    )


def _trunk(x, mem, p, prepared):
    batch, seq, d = x.shape
    mem_len = mem.shape[1]
    t = batch * seq
    x = x.reshape(t, d)
    mem = mem.reshape(batch * mem_len, d)
    depth = p["w_in"].shape[0]
    for i in range(depth):
        w = prepared[i]
        zxu = norm_matmul(x, p["norm_mix"][i], w["w_zxu"], BF16)
        dt_raw = norm_matmul(x, p["norm_mix"][i], w["w_dt"], F32)
        y_ssd = ssd_mixer(zxu, dt_raw, batch, seq, p["conv_w"][i], p["conv_b"][i], p["ssd_dt_bias"][i],
                          p["ssd_a_log"][i], p["ssd_d"][i], p["ssd_norm"][i], w["dims"])
        y_s5 = s5_mixer(zxu, w["u_blk"], batch, seq, w["tables"], p["s5_d"][i], p["s5_w_glu"][i],
                        p["s5_norm"][i])
        x = matmul_residual([y_ssd, y_s5], [w["w_out_ssd"], w["w_out_s5"]], x)
        q = norm_matmul(x, p["norm_attn"][i], w["w_q"], BF16)
        kv = norm_matmul(mem, p["norm_mem"][i], w["w_kv"], BF16)
        o = mem_attention(q, kv, batch, seq, mem_len)
        x = matmul_residual([o], [w["w_o"]], x)
        x = expert_choice_moe_norm(x, p["norm_ffn"][i], p["w_router"][i], w["w_gate"], w["w_up"],
                                   w["w_down"], p["norm_final"], i == depth - 1)
    return x.reshape(batch, seq, d)


def kernel(x_prompt, x_sample, mem_prompt, mem_sample, norm_mix, w_in, conv_w, conv_b, ssd_dt_bias,
           ssd_a_log, ssd_d, ssd_norm, s5_a_re, s5_a_im, s5_log_step, s5_b_re, s5_b_im, s5_c_re,
           s5_c_im, s5_d, s5_w_glu, s5_norm, w_out, norm_attn, norm_mem, w_q, w_k, w_v, w_o,
           norm_ffn, w_router, w_gate, w_up, w_down, norm_final):
    p = dict(norm_mix=norm_mix, w_in=w_in, conv_w=conv_w, conv_b=conv_b, ssd_dt_bias=ssd_dt_bias,
             ssd_a_log=ssd_a_log, ssd_d=ssd_d, ssd_norm=ssd_norm, s5_a_re=s5_a_re, s5_a_im=s5_a_im,
             s5_log_step=s5_log_step, s5_b_re=s5_b_re, s5_b_im=s5_b_im, s5_c_re=s5_c_re,
             s5_c_im=s5_c_im, s5_d=s5_d, s5_w_glu=s5_w_glu, s5_norm=s5_norm, w_out=w_out,
             norm_attn=norm_attn, norm_mem=norm_mem, w_q=w_q, w_k=w_k, w_v=w_v, w_o=w_o,
             norm_ffn=norm_ffn, w_router=w_router, w_gate=w_gate, w_up=w_up, w_down=w_down,
             norm_final=norm_final)
    prepared = [_prepare(p, i) for i in range(w_in.shape[0])]
    return (_trunk(x_prompt, mem_prompt, p, prepared), _trunk(x_sample, mem_sample, p, prepared))
```

```python
import functools

import jax
import jax.numpy as jnp
from jax import lax
from jax.experimental import pallas as pl
from jax.experimental.pallas import tpu as pltpu

F32 = jnp.float32
BF16 = jnp.bfloat16
I32 = jnp.int32
HIGHEST = lax.Precision.HIGHEST

EPS = 1e-6
SSD_GROUPS = 4
MEM_HEADS = 4
CAPACITY_FACTOR = 2
LANES = 128
HALO_ROWS = 16
SSD_CHUNK = 256
S5_CHUNK = 16
VMEM_LIMIT = 56 * 1024 * 1024


def _cparams(*sem):
    return pltpu.CompilerParams(dimension_semantics=sem, vmem_limit_bytes=VMEM_LIMIT)


def _tile(n, target):
    if n <= target:
        return n
    t = (target // LANES) * LANES
    while n % t:
        t -= LANES
    return t


def _rms(x, g):
    return x * lax.rsqrt(jnp.mean(x * x, axis=-1, keepdims=True) + EPS) * g


def _norm_mm_kernel(x_ref, g_ref, w_ref, *rest):
    w2_ref = o2_ref = None
    if len(rest) == 2:
        o_ref, h_scr = rest
    else:
        w2_ref, o_ref, o2_ref, h_scr = rest

    @pl.when(pl.program_id(1) == 0)
    def _():
        h = _rms(x_ref[...].astype(F32), g_ref[...]).astype(BF16)
        h_scr[...] = h
        if w2_ref is not None:
            o2_ref[...] = jnp.dot(h, w2_ref[...].astype(BF16), preferred_element_type=F32)

    o_ref[...] = jnp.dot(h_scr[...], w_ref[...].astype(BF16), preferred_element_type=F32).astype(o_ref.dtype)


def norm_matmul(x, g, w, out_dtype, w_side=None, tm=1024, tn=1024):
    t, d = x.shape
    n = w.shape[1]
    tm = _tile(t, tm)
    tn = _tile(n, tn)
    in_specs = [pl.BlockSpec((tm, d), lambda i, j: (i, 0)),
                pl.BlockSpec((1, d), lambda i, j: (0, 0)),
                pl.BlockSpec((d, tn), lambda i, j: (0, j))]
    out_specs = [pl.BlockSpec((tm, tn), lambda i, j: (i, j))]
    out_shape = [jax.ShapeDtypeStruct((t, n), out_dtype)]
    args = [x, g.reshape(1, d).astype(F32), w]
    if w_side is not None:
        ns = w_side.shape[1]
        in_specs.append(pl.BlockSpec((d, ns), lambda i, j: (0, 0)))
        out_specs.append(pl.BlockSpec((tm, ns), lambda i, j: (i, 0)))
        out_shape.append(jax.ShapeDtypeStruct((t, ns), F32))
        args.append(w_side)
    out = pl.pallas_call(
        _norm_mm_kernel,
        grid=(t // tm, n // tn),
        in_specs=in_specs,
        out_specs=out_specs,
        out_shape=out_shape,
        scratch_shapes=[pltpu.VMEM((tm, d), BF16)],
        compiler_params=_cparams("parallel", "arbitrary"),
        name="norm_matmul",
    )(*args)
    return out[0] if w_side is None else out


def _mm_res_kernel(*refs, n_a):
    a_refs = refs[:n_a]
    w_refs = refs[n_a:2 * n_a]
    r_ref, o_ref = refs[2 * n_a], refs[2 * n_a + 1]
    acc = r_ref[...].astype(F32)
    for a_ref, w_ref in zip(a_refs, w_refs):
        acc = acc + jnp.dot(a_ref[...], w_ref[...].astype(BF16), preferred_element_type=F32)
    o_ref[...] = acc


def matmul_residual(a_list, w_list, r, tm=1024, tn=1024):
    t, n = r.shape
    tm = _tile(t, tm)
    tn = _tile(n, tn)
    n_a = len(a_list)
    in_specs = [pl.BlockSpec((tm, a.shape[1]), lambda i, j: (i, 0)) for a in a_list]
    in_specs += [pl.BlockSpec((w.shape[0], tn), lambda i, j: (0, j)) for w in w_list]
    in_specs += [pl.BlockSpec((tm, tn), lambda i, j: (i, j))]
    return pl.pallas_call(
        functools.partial(_mm_res_kernel, n_a=n_a),
        grid=(t // tm, n // tn),
        in_specs=in_specs,
        out_specs=pl.BlockSpec((tm, tn), lambda i, j: (i, j)),
        out_shape=jax.ShapeDtypeStruct((t, n), F32),
        compiler_params=_cparams("parallel", "arbitrary"),
        name="matmul_residual",
    )(*a_list, *w_list, r)


def _pair_expand(v, h0, p):
    q = v.shape[0]
    lane = lax.broadcasted_iota(I32, (q, 2 * p), 1)
    return jnp.where(lane < p, v[:, h0:h0 + 1], v[:, h0 + 1:h0 + 2])


def _ssd_core(xs, bm, cm, dtr, bias, alog, state_ref, *, reverse, heads, hdim, groups, nstate):
    q = xs.shape[0]
    per_group = heads // groups
    assert per_group % 2 == 0 and 2 * hdim == LANES
    dt = jax.nn.softplus(dtr + bias)
    dta = dt * (-jnp.exp(alog))
    row = lax.broadcasted_iota(I32, (q, q), 0)
    col = lax.broadcasted_iota(I32, (q, q), 1)
    mask = (row <= col) if reverse else (row >= col)
    tri = jnp.where(mask, 1.0, 0.0).astype(BF16)
    cs = jnp.zeros_like(dta)
    rest = dta
    for _ in range(3):
        piece = rest.astype(BF16)
        cs = cs + jnp.dot(tri, piece, preferred_element_type=F32)
        rest = rest - piece.astype(F32)
    tot = cs[0:1, :] if reverse else cs[q - 1:q, :]
    cs_t = cs.T
    e_out = jnp.exp(cs)
    e_in = jnp.exp(tot - cs)
    e_tot = jnp.exp(tot)
    lane_lo = lax.broadcasted_iota(I32, (q, 2 * hdim), 1) < hdim
    ys = []
    for g in range(groups):
        b_g = bm[:, g * nstate:(g + 1) * nstate]
        c_g = cm[:, g * nstate:(g + 1) * nstate].astype(BF16)
        cb = lax.dot_general(c_g, b_g.astype(BF16), (((1,), (1,)), ((), ())),
                             preferred_element_type=F32)
        st = state_ref[g]
        y_off = jnp.dot(c_g, st.astype(BF16), preferred_element_type=F32)
        x_in, dec = [], []
        for pr in range(per_group // 2):
            h0 = g * per_group + 2 * pr
            c0 = h0 * hdim
            xdt = xs[:, c0:c0 + 2 * hdim] * _pair_expand(dt, h0, hdim)
            xdt16 = xdt.astype(BF16)
            y_pair = []
            for k in range(2):
                h = h0 + k
                seg = jnp.exp(jnp.where(mask, cs[:, h:h + 1] - cs_t[h:h + 1, :], -jnp.inf))
                m = (cb * seg).astype(BF16)
                y_pair.append(jnp.dot(m, xdt16, preferred_element_type=F32))
            y_d = jnp.where(lane_lo, y_pair[0], y_pair[1])
            o0 = 2 * pr * hdim
            ys.append(y_d + y_off[:, o0:o0 + 2 * hdim] * _pair_expand(e_out, h0, hdim))
            x_in.append((xdt * _pair_expand(e_in, h0, hdim)).astype(BF16))
            dec.append(_pair_expand(e_tot, h0, hdim))
        x_in = jnp.concatenate(x_in, axis=1) if len(x_in) > 1 else x_in[0]
        dec = jnp.concatenate(dec, axis=1) if len(dec) > 1 else dec[0]
        state_ref[g] = st * dec + jnp.dot(b_g.T.astype(BF16), x_in, preferred_element_type=F32)
    return jnp.concatenate(ys, axis=1)


def _ssd_fwd_kernel(xc_ref, xp_ref, xn_ref, dt_ref, cw_ref, cb_ref, bias_ref, alog_ref,
                    yf_ref, xact_ref, state_ref, *, dims):
    c = pl.program_id(1)
    nc = pl.num_programs(1)

    @pl.when(c == 0)
    def _():
        state_ref[...] = jnp.zeros_like(state_ref)

    x = xc_ref[...].astype(F32)
    q = x.shape[0]
    prev_row = jnp.where(c > 0, xp_ref[HALO_ROWS - 1:HALO_ROWS, :].astype(F32), 0.0)
    next_row = jnp.where(c < nc - 1, xn_ref[0:1, :].astype(F32), 0.0)
    rows = lax.broadcasted_iota(I32, (q, 1), 0)
    x_dn = jnp.where(rows == 0, prev_row, pltpu.roll(x, 1, 0))
    x_up = jnp.where(rows == q - 1, next_row, pltpu.roll(x, q - 1, 0))
    xc = cw_ref[0:1, :] * x_dn + cw_ref[1:2, :] * x + cw_ref[2:3, :] * x_up + cb_ref[...]
    xa = xc * jax.nn.sigmoid(xc)
    xact_ref[...] = xa.astype(BF16)
    w = dims["heads"] * dims["hdim"]
    gn = dims["groups"] * dims["nstate"]
    y = _ssd_core(xa[:, :w], xa[:, w:w + gn], xa[:, w + gn:], dt_ref[...], bias_ref[...],
                  alog_ref[...], state_ref, reverse=False, **dims)
    yf_ref[...] = y.astype(yf_ref.dtype)


def _ssd_bwd_kernel(xa_ref, dt_ref, yf_ref, z_ref, bias_ref, alog_ref, dsk_ref, ng_ref,
                    y_ref, state_ref, *, dims):
    @pl.when(pl.program_id(1) == 0)
    def _():
        state_ref[...] = jnp.zeros_like(state_ref)

    xa = xa_ref[...].astype(F32)
    w = dims["heads"] * dims["hdim"]
    gn = dims["groups"] * dims["nstate"]
    xs = xa[:, :w]
    yb = _ssd_core(xs, xa[:, w:w + gn], xa[:, w + gn:], dt_ref[...], bias_ref[...],
                   alog_ref[...], state_ref, reverse=True, **dims)
    y = yf_ref[...].astype(F32) + yb + dsk_ref[...] * xs
    z = z_ref[...].astype(F32)
    y = y * (z * jax.nn.sigmoid(z))
    gw = w // dims["groups"]
    parts = []
    for g in range(dims["groups"]):
        yg = y[:, g * gw:(g + 1) * gw]
        parts.append(yg * lax.rsqrt(jnp.mean(yg * yg, axis=-1, keepdims=True) + EPS))
    y_ref[...] = (jnp.concatenate(parts, axis=1) * ng_ref[...]).astype(y_ref.dtype)


def _pad_lanes(v):
    v = v.astype(F32)
    return jnp.pad(v, [(0, 0)] * (v.ndim - 1) + [(0, LANES - v.shape[-1])])


def ssd_mixer(zxu, dt_raw, batch, seq, conv_w, conv_b, dt_bias, a_log, d_skip, norm_g, dims):
    t = zxu.shape[0]
    w = dims["heads"] * dims["hdim"]
    xbc_w = conv_w.shape[-1]
    assert xbc_w % w == 0
    q = min(SSD_CHUNK, seq)
    nc = seq // q
    hb = q // HALO_ROWS
    nhalo = t // HALO_ROWS
    state_shape = (dims["groups"], dims["nstate"], (dims["heads"] // dims["groups"]) * dims["hdim"])
    bias = _pad_lanes(dt_bias)
    alog = _pad_lanes(a_log)
    full = lambda n: pl.BlockSpec((1, n), lambda b, c: (0, 0))
    vec = full(LANES)

    yf, xact = pl.pallas_call(
        functools.partial(_ssd_fwd_kernel, dims=dims),
        grid=(batch, nc),
        in_specs=[pl.BlockSpec((q, xbc_w), lambda b, c: (b * nc + c, 0)),
                  pl.BlockSpec((HALO_ROWS, xbc_w), lambda b, c: (jnp.maximum((b * nc + c) * hb - 1, 0), 0)),
                  pl.BlockSpec((HALO_ROWS, xbc_w), lambda b, c: (jnp.minimum((b * nc + c + 1) * hb, nhalo - 1), 0)),
                  pl.BlockSpec((q, LANES), lambda b, c: (b * nc + c, 0)),
                  pl.BlockSpec((3, xbc_w), lambda b, c: (0, 0)),
                  full(xbc_w), vec, vec],
        out_specs=[pl.BlockSpec((q, w), lambda b, c: (b * nc + c, 0)),
                   pl.BlockSpec((q, xbc_w), lambda b, c: (b * nc + c, 0))],
        out_shape=[jax.ShapeDtypeStruct((t, w), BF16), jax.ShapeDtypeStruct((t, xbc_w), BF16)],
        scratch_shapes=[pltpu.VMEM(state_shape, F32)],
        compiler_params=_cparams("parallel", "arbitrary"),
        name="ssd_forward",
    )(zxu, zxu, zxu, dt_raw, conv_w.astype(F32), conv_b.reshape(1, xbc_w).astype(F32), bias[0:1], alog[0:1])

    rev = lambda b, c: (b * nc + nc - 1 - c, 0)
    z_blk = xbc_w // w
    return pl.pallas_call(
        functools.partial(_ssd_bwd_kernel, dims=dims),
        grid=(batch, nc),
        in_specs=[pl.BlockSpec((q, xbc_w), rev),
                  pl.BlockSpec((q, LANES), rev),
                  pl.BlockSpec((q, w), rev),
                  pl.BlockSpec((q, w), lambda b, c: (b * nc + nc - 1 - c, z_blk)),
                  vec, vec, full(w), full(w)],
        out_specs=pl.BlockSpec((q, w), rev),
        out_shape=jax.ShapeDtypeStruct((t, w), BF16),
        scratch_shapes=[pltpu.VMEM(state_shape, F32)],
        compiler_params=_cparams("parallel", "arbitrary"),
        name="ssd_backward",
    )(xact, dt_raw, yf, zxu, bias[1:2], alog[1:2],
      jnp.repeat(d_skip.astype(F32), dims["hdim"]).reshape(1, w), norm_g.reshape(1, w).astype(F32))


def _s5_tables(a_re, a_im, log_step, b_re, b_im, c_re, c_im):
    qs = S5_CHUNK
    a_re, a_im, b_re, b_im, c_re, c_im = (v.astype(F32) for v in (a_re, a_im, b_re, b_im, c_re, c_im))
    _, g, p = a_re.shape
    cg = b_re.shape[-1]
    delta = jnp.exp(log_step.astype(F32))[..., None]
    mag = jnp.exp(a_re * delta)
    ar = mag * jnp.cos(a_im * delta)
    ai = mag * jnp.sin(a_im * delta)
    den = a_re * a_re + a_im * a_im
    qr = ((ar - 1.0) * a_re + ai * a_im) / den
    qi = (ai * a_re - (ar - 1.0) * a_im) / den
    bbr = qr[..., None] * b_re - qi[..., None] * b_im
    bbi = qr[..., None] * b_im + qi[..., None] * b_re
    pr, pi = [jnp.ones_like(ar)], [jnp.zeros_like(ar)]
    for _ in range(qs):
        pr.append(pr[-1] * ar - pi[-1] * ai)
        pi.append(pr[-2] * ai + pi[-1] * ar)
    pr = jnp.stack(pr)
    pi = jnp.stack(pi)
    cpr = c_re[None] * pr[:, :, :, None, :] - c_im[None] * pi[:, :, :, None, :]
    cpi = c_re[None] * pi[:, :, :, None, :] + c_im[None] * pr[:, :, :, None, :]
    wr = pr[..., None] * bbr[None] - pi[..., None] * bbi[None]
    wi = pr[..., None] * bbi[None] + pi[..., None] * bbr[None]
    kern = (jnp.einsum("tdgcp,dgpe->tdgce", cpr[:qs], bbr, precision=HIGHEST)
            - jnp.einsum("tdgcp,dgpe->tdgce", cpi[:qs], bbi, precision=HIGHEST))
    ii = jnp.arange(qs)[:, None]
    jj = jnp.arange(qs)[None, :]
    lag = ii - jj
    t_f = jnp.where((lag >= 0)[:, :, None, None, None], kern[jnp.clip(lag, 0, qs - 1), 0], 0.0)
    t_b = jnp.where((lag <= 0)[:, :, None, None, None], kern[jnp.clip(-lag, 0, qs - 1), 1], 0.0)
    toep = (t_f + t_b).transpose(2, 1, 4, 0, 3).reshape(g, qs * cg, qs * cg)
    def c_rows(d, powers):
        re = cpr[powers, d].transpose(1, 3, 0, 2)
        im = -cpi[powers, d].transpose(1, 3, 0, 2)
        return jnp.concatenate([re, im], axis=1).reshape(g, 2 * p, qs * cg)
    w_out = jnp.concatenate([toep, c_rows(0, jnp.arange(1, qs + 1)), c_rows(1, qs - jnp.arange(qs))], axis=1)
    def b_cols(d, powers):
        re = wr[powers, d].transpose(1, 0, 3, 2)
        im = wi[powers, d].transpose(1, 0, 3, 2)
        return jnp.concatenate([re, im], axis=3).reshape(g, qs * cg, 2 * p)
    w_state = jnp.concatenate([b_cols(0, qs - 1 - jnp.arange(qs)), b_cols(1, jnp.arange(qs))], axis=2)

    def multipliers(d):
        mul = jnp.concatenate([pr[qs, d], pr[qs, d]], axis=-1).reshape(1, g * 2 * p)
        swp = jnp.concatenate([-pi[qs, d], pi[qs, d]], axis=-1).reshape(1, g * 2 * p)
        return mul, swp
    return w_state.astype(BF16), w_out.astype(BF16), multipliers(0) + multipliers(1)


S5_GROUP_BLOCK = 8


def _s5_state_kernel(u_ref, w_ref, vf_ref, vb_ref):
    half = w_ref.shape[2] // 2
    for k in range(u_ref.shape[0]):
        v = jnp.dot(u_ref[k], w_ref[k], preferred_element_type=F32)
        vf_ref[:, k * half:(k + 1) * half] = v[:, :half]
        vb_ref[:, k * half:(k + 1) * half] = v[:, half:]


def _s5_scan_kernel(v_ref, mul_ref, swp_ref, o_ref, s_scr, *, reverse):
    @pl.when(pl.program_id(1) == 0)
    def _():
        s_scr[...] = jnp.zeros_like(s_scr)

    s = s_scr[...]
    mul = mul_ref[...]
    swp = swp_ref[...]
    steps = v_ref.shape[1]
    width = s.shape[1]
    re_lane = (lax.broadcasted_iota(I32, (1, width), 1) % LANES) < (LANES // 2)
    order = range(steps - 1, -1, -1) if reverse else range(steps)
    for k in order:
        o_ref[:, k, :] = s
        partner = jnp.where(re_lane, pltpu.roll(s, width - LANES // 2, 1), pltpu.roll(s, LANES // 2, 1))
        s = mul * s + swp * partner + v_ref[:, k, :]
    s_scr[...] = s


def _s5_out_kernel(u_ref, sf_ref, sb_ref, w_ref, y_ref):
    qc = u_ref.shape[2]
    ns = (w_ref.shape[1] - qc) // 2
    for k in range(u_ref.shape[0]):
        sf = sf_ref[:, k * ns:(k + 1) * ns].astype(BF16)
        sb = sb_ref[:, k * ns:(k + 1) * ns].astype(BF16)
        y = jnp.dot(u_ref[k], w_ref[k, :qc, :], preferred_element_type=F32)
        y = y + jnp.dot(sf, w_ref[k, qc:qc + ns, :], preferred_element_type=F32)
        y = y + jnp.dot(sb, w_ref[k, qc + ns:, :], preferred_element_type=F32)
        y_ref[k] = y.astype(y_ref.dtype)


def _s5_post_kernel(y_ref, u_ref, d_ref, w_ref, g_ref, o_ref):
    y = y_ref[...].astype(F32) + d_ref[...] * u_ref[...].astype(F32)
    y = jax.nn.gelu(y)
    gate = jax.nn.sigmoid(jnp.dot(y.astype(BF16), w_ref[...].astype(BF16), preferred_element_type=F32))
    o_ref[...] = _rms(y * gate, g_ref[...]).astype(o_ref.dtype)


def _group_lane_blocks(n, cg):
    return lax.broadcasted_iota(I32, (n, LANES), 1) // cg


def _s5_split_kernel(u_ref, o_ref, *, groups, cg):
    n = u_ref.shape[0]
    per_vreg = LANES // cg
    width = groups * cg
    blk = _group_lane_blocks(n, cg)
    for v in range(groups // per_vreg):
        rolled = {}
        for j in range(S5_CHUNK):
            src = u_ref[:, j * width + v * LANES:j * width + (v + 1) * LANES].astype(F32)
            for k in range(per_vreg):
                rolled[j, k] = src if k == 0 else pltpu.roll(src, k * cg, 1)
        for gl in range(per_vreg):
            for h in range(S5_CHUNK // per_vreg):
                acc = rolled[h * per_vreg, (-gl) % per_vreg]
                for jj in range(1, per_vreg):
                    acc = jnp.where(blk == jj, rolled[h * per_vreg + jj, (jj - gl) % per_vreg], acc)
                o_ref[v * per_vreg + gl, :, h * LANES:(h + 1) * LANES] = acc.astype(o_ref.dtype)


def _s5_merge_kernel(y_ref, o_ref, *, groups, cg):
    n = y_ref.shape[1]
    per_vreg = LANES // cg
    width = groups * cg
    blk = _group_lane_blocks(n, cg)
    for v in range(groups // per_vreg):
        rolled = {}
        for gl in range(per_vreg):
            for h in range(S5_CHUNK // per_vreg):
                src = y_ref[v * per_vreg + gl, :, h * LANES:(h + 1) * LANES].astype(F32)
                for k in range(per_vreg):
                    rolled[gl, h, k] = src if k == 0 else pltpu.roll(src, k * cg, 1)
        for i in range(S5_CHUNK):
            h, ii = divmod(i, per_vreg)
            acc = rolled[0, h, (-ii) % per_vreg]
            for gl in range(1, per_vreg):
                acc = jnp.where(blk == gl, rolled[gl, h, (gl - ii) % per_vreg], acc)
            o_ref[:, i * width + v * LANES:i * width + (v + 1) * LANES] = acc.astype(o_ref.dtype)


def s5_mixer(zxu, u_blk, batch, seq, tables, d_skip, w_glu, norm_g):
    w_state, w_out, (mul_f, swp_f, mul_b, swp_b) = tables
    t = zxu.shape[0]
    g, qc, ns2 = w_state.shape
    ns = ns2 // 2
    cg = qc // S5_CHUNK
    width = g * cg
    nchunk = seq // S5_CHUNK
    rows = nchunk * batch
    u = zxu[:, u_blk * width:(u_blk + 1) * width]
    u_rows = u.reshape(rows, S5_CHUNK * width)
    rn = min(64, rows)
    ug = pl.pallas_call(
        functools.partial(_s5_split_kernel, groups=g, cg=cg),
        grid=(rows // rn,),
        in_specs=[pl.BlockSpec((rn, S5_CHUNK * width), lambda r: (r, 0))],
        out_specs=pl.BlockSpec((g, rn, qc), lambda r: (0, r, 0)),
        out_shape=jax.ShapeDtypeStruct((g, rows, qc), BF16),
        compiler_params=_cparams("parallel"),
        name="s5_split",
    )(u_rows)
    tr = _tile(rows, 1024)
    gb = min(S5_GROUP_BLOCK, g)
    vf, vb = pl.pallas_call(
        _s5_state_kernel,
        grid=(g // gb, rows // tr),
        in_specs=[pl.BlockSpec((gb, tr, qc), lambda i, r: (i, r, 0)),
                  pl.BlockSpec((gb, qc, ns2), lambda i, r: (i, 0, 0))],
        out_specs=[pl.BlockSpec((tr, gb * ns), lambda i, r: (r, i))] * 2,
        out_shape=[jax.ShapeDtypeStruct((rows, g * ns), F32)] * 2,
        compiler_params=_cparams("parallel", "parallel"),
        name="s5_state_in",
    )(ug, w_state)

    panel = _tile(g * ns, 4096)
    steps = min(32, nchunk)
    nblk = nchunk // steps

    def scan(v, mul, swp, reverse):
        rmap = (lambda pnl, i: (0, nblk - 1 - i, pnl)) if reverse else (lambda pnl, i: (0, i, pnl))
        return pl.pallas_call(
            functools.partial(_s5_scan_kernel, reverse=reverse),
            grid=(g * ns // panel, nblk),
            in_specs=[pl.BlockSpec((batch, steps, panel), rmap),
                      pl.BlockSpec((1, panel), lambda pnl, i: (0, pnl)),
                      pl.BlockSpec((1, panel), lambda pnl, i: (0, pnl))],
            out_specs=pl.BlockSpec((batch, steps, panel), rmap),
            out_shape=jax.ShapeDtypeStruct((batch, nchunk, g * ns), F32),
            scratch_shapes=[pltpu.VMEM((batch, panel), F32)],
            compiler_params=_cparams("parallel", "arbitrary"),
            name="s5_scan_bwd" if reverse else "s5_scan_fwd",
        )(v.reshape(batch, nchunk, g * ns), mul, swp).reshape(rows, g * ns)

    sf = scan(vf, mul_f, swp_f, False)
    sb = scan(vb, mul_b, swp_b, True)
    yg = pl.pallas_call(
        _s5_out_kernel,
        grid=(g // gb, rows // tr),
        in_specs=[pl.BlockSpec((gb, tr, qc), lambda i, r: (i, r, 0)),
                  pl.BlockSpec((tr, gb * ns), lambda i, r: (r, i)),
                  pl.BlockSpec((tr, gb * ns), lambda i, r: (r, i)),
                  pl.BlockSpec((gb, qc + 2 * ns, qc), lambda i, r: (i, 0, 0))],
        out_specs=pl.BlockSpec((gb, tr, qc), lambda i, r: (i, r, 0)),
        out_shape=jax.ShapeDtypeStruct((g, rows, qc), BF16),
        compiler_params=_cparams("parallel", "parallel"),
        name="s5_state_out",
    )(ug, sf, sb, w_out)
    y_rows = pl.pallas_call(
        functools.partial(_s5_merge_kernel, groups=g, cg=cg),
        grid=(rows // rn,),
        in_specs=[pl.BlockSpec((g, rn, qc), lambda r: (0, r, 0))],
        out_specs=pl.BlockSpec((rn, S5_CHUNK * width), lambda r: (r, 0)),
        out_shape=jax.ShapeDtypeStruct((rows, S5_CHUNK * width), BF16),
        compiler_params=_cparams("parallel"),
        name="s5_merge",
    )(yg)
    y = y_rows.reshape(t, width)
    tm = _tile(t, 1024)
    return pl.pallas_call(
        _s5_post_kernel,
        grid=(t // tm,),
        in_specs=[pl.BlockSpec((tm, width), lambda i: (i, 0)),
                  pl.BlockSpec((tm, width), lambda i: (i, u_blk)),
                  pl.BlockSpec((1, width), lambda i: (0, 0)),
                  pl.BlockSpec((width, width), lambda i: (0, 0)),
                  pl.BlockSpec((1, width), lambda i: (0, 0))],
        out_specs=pl.BlockSpec((tm, width), lambda i: (i, 0)),
        out_shape=jax.ShapeDtypeStruct((t, width), BF16),
        compiler_params=_cparams("parallel"),
        name="s5_post",
    )(y, zxu, d_skip.reshape(1, width).astype(F32), w_glu.astype(F32), norm_g.reshape(1, width).astype(F32))


def _attn_kernel(q_ref, kv_ref, o_ref, *, heads):
    d = q_ref.shape[1]
    hd = d // heads
    scale = hd ** -0.5
    for h in range(heads):
        qh = q_ref[:, h * hd:(h + 1) * hd]
        kh = kv_ref[:, h * hd:(h + 1) * hd]
        vh = kv_ref[:, d + h * hd:d + (h + 1) * hd]
        s = lax.dot_general(qh, kh, (((1,), (1,)), ((), ())), preferred_element_type=F32) * scale
        p = jnp.exp(s - jnp.max(s, axis=-1, keepdims=True))
        o = jnp.dot(p.astype(BF16), vh, preferred_element_type=F32) / jnp.sum(p, axis=-1, keepdims=True)
        o_ref[:, h * hd:(h + 1) * hd] = o.astype(o_ref.dtype)


def mem_attention(q, kv, batch, seq, mem_len):
    t, d = q.shape
    tq = _tile(seq, 1024)
    nq = seq // tq
    return pl.pallas_call(
        functools.partial(_attn_kernel, heads=MEM_HEADS),
        grid=(batch, nq),
        in_specs=[pl.BlockSpec((tq, d), lambda b, i: (b * nq + i, 0)),
                  pl.BlockSpec((mem_len, 2 * d), lambda b, i: (b, 0))],
        out_specs=pl.BlockSpec((tq, d), lambda b, i: (b * nq + i, 0)),
        out_shape=jax.ShapeDtypeStruct((t, d), BF16),
        compiler_params=_cparams("parallel", "arbitrary"),
        name="mem_attention",
    )(q, kv)


def _router_kernel(x_ref, g_ref, wr_ref, hn_ref, p_ref):
    hn = _rms(x_ref[...], g_ref[...])
    hn_hi = hn.astype(BF16)
    hn_ref[...] = hn_hi
    hn_lo = (hn - hn_hi.astype(F32)).astype(BF16)
    w = wr_ref[...]
    w_hi = w.astype(BF16)
    w_lo = (w - w_hi.astype(F32)).astype(BF16)
    nt = lambda a, b: lax.dot_general(a, b, (((1,), (1,)), ((), ())), preferred_element_type=F32)
    logits = nt(w_hi, hn_hi) + nt(w_lo, hn_hi) + nt(w_hi, hn_lo)
    e = jnp.exp(logits - jnp.max(logits, axis=0, keepdims=True))
    p_ref[...] = e / jnp.sum(e, axis=0, keepdims=True)


def router(x, g, w_router, tm=1024):
    t, d = x.shape
    e = w_router.shape[1]
    tm = _tile(t, tm)
    return pl.pallas_call(
        _router_kernel,
        grid=(t // tm,),
        in_specs=[pl.BlockSpec((tm, d), lambda i: (i, 0)),
                  pl.BlockSpec((1, d), lambda i: (0, 0)),
                  pl.BlockSpec((e, d), lambda i: (0, 0))],
        out_specs=[pl.BlockSpec((tm, d), lambda i: (i, 0)),
                   pl.BlockSpec((e, tm), lambda i: (0, i))],
        out_shape=[jax.ShapeDtypeStruct((t, d), BF16), jax.ShapeDtypeStruct((e, t), F32)],
        compiler_params=_cparams("parallel"),
        name="router",
    )(x, g.reshape(1, d).astype(F32), w_router.T.astype(F32))


def _select_kernel(p_ref, pos_ref, gate_ref, cnt_ref, *, cap, tb):
    e, t = p_ref.shape
    bits = pltpu.bitcast(p_ref[...], I32)

    def search(i, prefix):
        cand = prefix | jnp.left_shift(jnp.int32(1), 30 - i)
        cnt = jnp.sum((bits >= cand).astype(I32), axis=1, keepdims=True)
        return jnp.where(cnt >= cap, cand, prefix)

    thr = lax.fori_loop(0, 31, search, jnp.zeros((e, 1), I32))
    need = (cap - jnp.sum((bits > thr).astype(I32), axis=1, keepdims=True)).astype(F32)
    r = lax.broadcasted_iota(I32, (tb, tb), 0)
    c = lax.broadcasted_iota(I32, (tb, tb), 1)
    incl = (r <= c).astype(BF16)
    strict = (r < c).astype(BF16)

    def block(k, carry):
        carry_eq, carry_sel = carry
        sl = pl.ds(pl.multiple_of(k * tb, tb), tb)
        p = p_ref[:, sl]
        b = pltpu.bitcast(p, I32)
        eq = b == thr
        eq16 = jnp.where(eq, 1.0, 0.0).astype(BF16)
        rank = jnp.dot(eq16, incl, preferred_element_type=F32) + carry_eq
        sel = (b > thr) | (eq & (rank <= need))
        sel16 = jnp.where(sel, 1.0, 0.0).astype(BF16)
        pos = jnp.dot(sel16, strict, preferred_element_type=F32) + carry_sel
        pos_ref[:, sl] = jnp.where(sel, pos, -1.0)
        gate_ref[:, sl] = jnp.where(sel, p, 0.0)
        cnt_ref[k] = jnp.broadcast_to(carry_sel, (e, LANES))
        return (carry_eq + jnp.sum(eq16.astype(F32), axis=1, keepdims=True),
                carry_sel + jnp.sum(sel16.astype(F32), axis=1, keepdims=True))

    lax.fori_loop(0, t // tb, block, (jnp.zeros((e, 1), F32), jnp.zeros((e, 1), F32)))


def select_topk(probs_t, cap, tb):
    e, t = probs_t.shape
    nb = t // tb
    return pl.pallas_call(
        functools.partial(_select_kernel, cap=cap, tb=tb),
        grid=(1,),
        in_specs=[pl.BlockSpec((e, t), lambda i: (0, 0))],
        out_specs=[pl.BlockSpec((e, t), lambda i: (0, 0)),
                   pl.BlockSpec((e, t), lambda i: (0, 0)),
                   pl.BlockSpec((nb, e, LANES), lambda i: (0, 0, 0))],
        out_shape=[jax.ShapeDtypeStruct((e, t), F32), jax.ShapeDtypeStruct((e, t), F32),
                   jax.ShapeDtypeStruct((nb, e, LANES), F32)],
        compiler_params=_cparams("arbitrary"),
        name="select_topk",
    )(probs_t)


GATHER_WINDOW = 192


def _gather_kernel(a0_ref, nw_ref, pos_ref, hn_ref, xe_ref, *, cap, win):
    e = pl.program_id(0)
    kb = pl.program_id(1)

    @pl.when(kb == 0)
    def _():
        xe_ref[...] = jnp.zeros_like(xe_ref)

    item = e * pl.num_programs(1) + kb
    tb = hn_ref.shape[0]
    slot = pos_ref[pl.ds(e, 1), :]
    local = lax.broadcasted_iota(I32, (win, tb), 0).astype(F32)

    def window(w, carry):
        start = a0_ref[item] * ROW_ALIGN + w * win
        base = jnp.minimum(a0_ref[item] + w * (win // ROW_ALIGN), (cap - win) // ROW_ALIGN) * ROW_ALIGN
        hit = (slot - base.astype(F32) == local) & (slot >= start.astype(F32))
        rows = jnp.dot(jnp.where(hit, 1.0, 0.0).astype(BF16), hn_ref[...], preferred_element_type=F32)
        sl = pl.ds(pl.multiple_of(base, ROW_ALIGN), win)
        xe_ref[sl, :] = xe_ref[sl, :] + rows.astype(xe_ref.dtype)
        return carry

    lax.fori_loop(0, nw_ref[item], window, 0)


def moe_gather(hn, pos_t, bounds, cap, tb):
    t, d = hn.shape
    e = pos_t.shape[0]
    win = min(GATHER_WINDOW, cap)
    assert cap % ROW_ALIGN == 0 and win % ROW_ALIGN == 0
    c0, c1 = bounds[:, :-1], bounds[:, 1:]
    a0 = c0 // ROW_ALIGN
    nw = jnp.where(c1 > c0, (c1 - a0 * ROW_ALIGN + win - 1) // win, 0)
    grid_spec = pltpu.PrefetchScalarGridSpec(
        num_scalar_prefetch=2,
        grid=(e, t // tb),
        in_specs=[pl.BlockSpec((e, tb), lambda x, k, a0, nw: (0, k)),
                  pl.BlockSpec((tb, d), lambda x, k, a0, nw: (k, 0))],
        out_specs=pl.BlockSpec((cap, d), lambda x, k, a0, nw: (x, 0)),
    )
    return pl.pallas_call(
        functools.partial(_gather_kernel, cap=cap, win=win),
        grid_spec=grid_spec,
        out_shape=jax.ShapeDtypeStruct((e * cap, d), BF16),
        compiler_params=_cparams("parallel", "arbitrary"),
        name="moe_gather",
    )(a0.reshape(-1).astype(I32), nw.reshape(-1).astype(I32), pos_t, hn)


def _ffn_kernel(xe_ref, wg_ref, wu_ref, wd_ref, ye_ref, acc_ref):
    f = pl.program_id(2)

    @pl.when(f == 0)
    def _():
        acc_ref[...] = jnp.zeros_like(acc_ref)

    xe = xe_ref[...]
    gate = jnp.dot(xe, wg_ref[0].astype(BF16), preferred_element_type=F32)
    up = jnp.dot(xe, wu_ref[0].astype(BF16), preferred_element_type=F32)
    he = (gate * jax.nn.sigmoid(gate) * up).astype(BF16)
    acc_ref[...] += jnp.dot(he, wd_ref[0].astype(BF16), preferred_element_type=F32)

    @pl.when(f == pl.num_programs(2) - 1)
    def _():
        ye_ref[...] = acc_ref[...].astype(ye_ref.dtype)


def moe_ffn(xe, w_gate, w_up, w_down, cap, ts=1024, tf=256):
    n, d = xe.shape
    e, _, ff = w_gate.shape
    ts = _tile(cap, ts)
    tf = _tile(ff, tf)
    nsb = cap // ts
    return pl.pallas_call(
        _ffn_kernel,
        grid=(e, nsb, ff // tf),
        in_specs=[pl.BlockSpec((ts, d), lambda x, s, f: (x * nsb + s, 0)),
                  pl.BlockSpec((1, d, tf), lambda x, s, f: (x, 0, f)),
                  pl.BlockSpec((1, d, tf), lambda x, s, f: (x, 0, f)),
                  pl.BlockSpec((1, tf, d), lambda x, s, f: (x, f, 0))],
        out_specs=pl.BlockSpec((ts, d), lambda x, s, f: (x * nsb + s, 0)),
        out_shape=jax.ShapeDtypeStruct((n, d), BF16),
        scratch_shapes=[pltpu.VMEM((ts, d), F32)],
        compiler_params=_cparams("parallel", "parallel", "arbitrary"),
        name="moe_ffn",
    )(xe, w_gate, w_up, w_down)


SLOT_SPLIT = 64
SCATTER_WINDOW = 128
SCATTER_WINDOWS = 8
ROW_ALIGN = 16
NO_SLOT = 1 << 24


def _scatter_kernel(kb_ref, first_ref, last_ref, real_ref, ex_ref, off_ref, base_ref, vlo_ref,
                    tok_ref, *rest, experts, final_norm):
    ye_refs = rest[:SCATTER_WINDOWS]
    x_ref, g_ref, y_ref, acc_ref = rest[SCATTER_WINDOWS:]
    i = pl.program_id(0)
    win = SCATTER_WINDOW

    @pl.when(first_ref[i] == 1)
    def _():
        acc_ref[...] = jnp.zeros_like(acc_ref)

    @pl.when(real_ref[i] == 1)
    def _():
        tok = tok_ref[...]
        kdim = tok.shape[1]
        e = experts
        row = lax.broadcasted_iota(I32, (kdim, 2 * win), 0)
        col = lax.broadcasted_iota(I32, (kdim, 2 * win), 1)
        lane = lax.broadcasted_iota(I32, (1, win), 1).astype(F32)
        pieces = []
        for k in range(SCATTER_WINDOWS):
            w = i * SCATTER_WINDOWS + k
            ek = ex_ref[w]
            sel = jnp.where((row == ek) & (col < win), float(SLOT_SPLIT), 0.0)
            sel = sel + jnp.where((row == e + ek) & (col < win), 1.0, 0.0)
            sel = sel + jnp.where((row == 2 * e + ek) & (col >= win), 1.0, 0.0)
            both = jnp.dot(tok, sel.astype(BF16), preferred_element_type=F32)
            slot = both[:, :win]
            hit = (slot - base_ref[w].astype(F32) == lane) & (slot >= vlo_ref[w].astype(F32))
            pieces.append(jnp.where(hit, both[:, win:], 0.0).astype(BF16))
        onehot = jnp.concatenate(pieces, axis=1)
        rows = jnp.concatenate([r[...] for r in ye_refs], axis=0)
        acc_ref[...] += jnp.dot(onehot, rows, preferred_element_type=F32)

    @pl.when(last_ref[i] == 1)
    def _():
        y = x_ref[...] + acc_ref[...]
        y_ref[...] = _rms(y, g_ref[...]) if final_norm else y


def _scatter_items(bounds, cap, tb):
    e, nkb1 = bounds.shape
    nkb = nkb1 - 1
    win, grp = SCATTER_WINDOW, SCATTER_WINDOWS
    c0, c1 = bounds[:, :-1].T, bounds[:, 1:].T
    a0 = (c0 // ROW_ALIGN) * ROW_ALIGN
    nwin = jnp.where(c1 > c0, (c1 - a0 + win - 1) // win, 0)
    wmax = (tb + ROW_ALIGN - 1 + win - 1) // win
    w = jnp.arange(wmax, dtype=I32)
    valid = (w[None, None, :] < nwin[:, :, None]).reshape(-1)
    start = (a0[:, :, None] + w * win).reshape(-1)
    base = jnp.minimum(start, cap - win)
    expert = jnp.broadcast_to(jnp.arange(e, dtype=I32)[None, :, None], (nkb, e, wmax)).reshape(-1)
    block = jnp.broadcast_to(jnp.arange(nkb, dtype=I32)[:, None, None], (nkb, e, wmax)).reshape(-1)
    per_block = jnp.sum(valid.reshape(nkb, -1).astype(I32), axis=1)
    items_per_block = jnp.maximum((per_block + grp - 1) // grp, 1)
    item_end = jnp.cumsum(items_per_block)
    item_start = item_end - items_per_block
    rank = jnp.cumsum(valid.astype(I32)) - 1 - (jnp.cumsum(per_block) - per_block)[block]
    max_windows = e * cap // win + (e * nkb * (win + ROW_ALIGN - 2) + win - 1) // win
    max_items = (max_windows + grp - 1) // grp + nkb
    dest = jnp.where(valid, (item_start[block] + rank // grp) * grp + rank % grp, max_items * grp)

    fill = jnp.broadcast_to(jnp.array([0, 0, NO_SLOT, 0], I32), (max_items * grp, 4))
    values = jnp.stack([expert, (expert * cap + base) // ROW_ALIGN, base, start], axis=1).astype(I32)
    placed = fill.at[dest].set(values, mode="drop")
    it = jnp.arange(max_items, dtype=I32)
    real = it < item_end[-1]
    kb = jnp.minimum(jnp.sum((it[:, None] >= item_end[None, :]).astype(I32), axis=1), nkb - 1)
    first = real & (it == item_start[kb])
    last = real & (it == item_end[kb] - 1)
    return (kb, first.astype(I32), last.astype(I32), real.astype(I32),
            placed[:, 0], placed[:, 1], placed[:, 2], placed[:, 3])


def moe_scatter_norm(ye, pos_t, gate_t, x, g_final, final_norm, bounds, cap, tb):
    t, d = x.shape
    e = pos_t.shape[0]
    assert 3 * e <= LANES and cap % SCATTER_WINDOW == 0
    pos_c = pos_t.T
    hi = jnp.floor(pos_c / SLOT_SPLIT)
    tok = jnp.concatenate([hi, pos_c - SLOT_SPLIT * hi, gate_t.T, jnp.zeros((t, LANES - 3 * e), F32)],
                          axis=1).astype(BF16)
    items = _scatter_items(bounds, cap, tb)
    grp = SCATTER_WINDOWS
    tokmap = lambda i, kb, *_: (kb[i], 0)

    def window(k):
        return pl.BlockSpec((pl.Element(SCATTER_WINDOW), pl.Element(d)),
                            lambda i, kb, f, l, r, ex, off, base, vlo: (off[i * grp + k] * ROW_ALIGN, 0))

    grid_spec = pltpu.PrefetchScalarGridSpec(
        num_scalar_prefetch=8,
        grid=(items[0].shape[0],),
        in_specs=[pl.BlockSpec((tb, LANES), tokmap)] + [window(k) for k in range(grp)]
                 + [pl.BlockSpec((tb, d), tokmap), pl.BlockSpec((1, d), lambda i, *_: (0, 0))],
        out_specs=pl.BlockSpec((tb, d), tokmap),
        scratch_shapes=[pltpu.VMEM((tb, d), F32)],
    )
    return pl.pallas_call(
        functools.partial(_scatter_kernel, experts=e, final_norm=final_norm),
        grid_spec=grid_spec,
        out_shape=jax.ShapeDtypeStruct((t, d), F32),
        compiler_params=_cparams("arbitrary"),
        name="moe_scatter_norm",
    )(*items, tok, *([ye] * grp), x, g_final.reshape(1, d).astype(F32))


def expert_choice_moe_norm(x, norm_g, w_router, w_gate, w_up, w_down, g_final, final_norm):
    t, d = x.shape
    e = w_router.shape[1]
    cap = CAPACITY_FACTOR * t // e
    sel_tb = _tile(t, 512)
    hn, probs_t = router(x, norm_g, w_router)
    pos_t, gate_t, cnt = select_topk(probs_t, cap, sel_tb)
    counts = cnt[:, :, 0].T.astype(I32)
    bounds = jnp.concatenate([counts, jnp.full((e, 1), cap, I32)], axis=1)
    g_tb = _tile(t, 1024)
    xe = moe_gather(hn, pos_t, bounds[:, ::g_tb // sel_tb], cap, g_tb)
    ye = moe_ffn(xe, w_gate, w_up, w_down, cap)
    return moe_scatter_norm(ye, pos_t, gate_t, x, g_final, final_norm, bounds, cap, sel_tb)


def _prepare(p, depth_i):
    i = depth_i
    ssd_w = p["ssd_norm"].shape[-1]
    xbc_w = p["conv_w"].shape[-1]
    heads = p["ssd_a_log"].shape[-1]
    w_in = p["w_in"][i]
    o_z, o_x, o_dt = ssd_w, ssd_w + xbc_w, ssd_w + xbc_w + heads
    w_zxu = jnp.concatenate([w_in[:, o_z:o_x], w_in[:, :o_z], w_in[:, o_dt:]], axis=1).astype(F32)
    w_dt = _pad_lanes(w_in[:, o_x:o_dt])
    dims = dict(heads=heads, hdim=ssd_w // heads, groups=SSD_GROUPS,
                nstate=(xbc_w - ssd_w) // (2 * SSD_GROUPS))
    tables = _s5_tables(p["s5_a_re"][i], p["s5_a_im"][i], p["s5_log_step"][i], p["s5_b_re"][i],
                        p["s5_b_im"][i], p["s5_c_re"][i], p["s5_c_im"][i])
    w_out = p["w_out"][i].astype(F32)
    return dict(
        w_zxu=w_zxu, w_dt=w_dt, dims=dims, tables=tables, u_blk=(xbc_w + ssd_w) // (w_in.shape[1] - o_dt),
        w_out_ssd=w_out[:ssd_w], w_out_s5=w_out[ssd_w:],
        w_q=p["w_q"][i].astype(F32),
        w_kv=jnp.concatenate([p["w_k"][i], p["w_v"][i]], axis=1).astype(F32),
        w_o=p["w_o"][i].astype(F32),
        w_gate=p["w_gate"][i], w_up=p["w_up"][i], w_down=p["w_down"][i],
    )


def _trunk(x, mem, p, prepared):
    batch, seq, d = x.shape
    mem_len = mem.shape[1]
    t = batch * seq
    x = x.reshape(t, d)
    mem = mem.reshape(batch * mem_len, d)
    depth = p["w_in"].shape[0]
    for i in range(depth):
        w = prepared[i]
        zxu, dt_raw = norm_matmul(x, p["norm_mix"][i], w["w_zxu"], BF16, w_side=w["w_dt"])
        y_ssd = ssd_mixer(zxu, dt_raw, batch, seq, p["conv_w"][i], p["conv_b"][i], p["ssd_dt_bias"][i],
                          p["ssd_a_log"][i], p["ssd_d"][i], p["ssd_norm"][i], w["dims"])
        y_s5 = s5_mixer(zxu, w["u_blk"], batch, seq, w["tables"], p["s5_d"][i], p["s5_w_glu"][i],
                        p["s5_norm"][i])
        x = matmul_residual([y_ssd, y_s5], [w["w_out_ssd"], w["w_out_s5"]], x)
        q = norm_matmul(x, p["norm_attn"][i], w["w_q"], BF16)
        kv = norm_matmul(mem, p["norm_mem"][i], w["w_kv"], BF16)
        o = mem_attention(q, kv, batch, seq, mem_len)
        x = matmul_residual([o], [w["w_o"]], x)
        x = expert_choice_moe_norm(x, p["norm_ffn"][i], p["w_router"][i], w["w_gate"], w["w_up"],
                                   w["w_down"], p["norm_final"], i == depth - 1)
    return x.reshape(batch, seq, d)


def kernel(x_prompt, x_sample, mem_prompt, mem_sample, norm_mix, w_in, conv_w, conv_b, ssd_dt_bias,
           ssd_a_log, ssd_d, ssd_norm, s5_a_re, s5_a_im, s5_log_step, s5_b_re, s5_b_im, s5_c_re,
           s5_c_im, s5_d, s5_w_glu, s5_norm, w_out, norm_attn, norm_mem, w_q, w_k, w_v, w_o,
           norm_ffn, w_router, w_gate, w_up, w_down, norm_final):
    p = dict(norm_mix=norm_mix, w_in=w_in, conv_w=conv_w, conv_b=conv_b, ssd_dt_bias=ssd_dt_bias,
             ssd_a_log=ssd_a_log, ssd_d=ssd_d, ssd_norm=ssd_norm, s5_a_re=s5_a_re, s5_a_im=s5_a_im,
             s5_log_step=s5_log_step, s5_b_re=s5_b_re, s5_b_im=s5_b_im, s5_c_re=s5_c_re,
             s5_c_im=s5_c_im, s5_d=s5_d, s5_w_glu=s5_w_glu, s5_norm=s5_norm, w_out=w_out,
             norm_attn=norm_attn, norm_mem=norm_mem, w_q=w_q, w_k=w_k, w_v=w_v, w_o=w_o,
             norm_ffn=norm_ffn, w_router=w_router, w_gate=w_gate, w_up=w_up, w_down=w_down,
             norm_final=norm_final)
    prepared = [_prepare(p, i) for i in range(w_in.shape[0])]
    return (_trunk(x_prompt, mem_prompt, p, prepared), _trunk(x_sample, mem_sample, p, prepared))
```

```python
import functools

import jax
import jax.numpy as jnp
from jax import lax
from jax.experimental import pallas as pl
from jax.experimental.pallas import tpu as pltpu

F32 = jnp.float32
BF16 = jnp.bfloat16
I32 = jnp.int32
HIGHEST = lax.Precision.HIGHEST

EPS = 1e-6
SSD_GROUPS = 4
MEM_HEADS = 4
CAPACITY_FACTOR = 2
LANES = 128
HALO_ROWS = 16
SSD_CHUNK = 256
S5_CHUNK = 16
VMEM_LIMIT = 56 * 1024 * 1024


def _cparams(*sem):
    return pltpu.CompilerParams(dimension_semantics=sem, vmem_limit_bytes=VMEM_LIMIT)


def _tile(n, target):
    if n <= target:
        return n
    t = (target // LANES) * LANES
    while n % t:
        t -= LANES
    return t


def _rms(x, g):
    return x * lax.rsqrt(jnp.mean(x * x, axis=-1, keepdims=True) + EPS) * g


def _norm_mm_kernel(x_ref, g_ref, w_ref, *rest):
    w2_ref = o2_ref = None
    if len(rest) == 2:
        o_ref, h_scr = rest
    else:
        w2_ref, o_ref, o2_ref, h_scr = rest

    @pl.when(pl.program_id(1) == 0)
    def _():
        h = _rms(x_ref[...].astype(F32), g_ref[...]).astype(BF16)
        h_scr[...] = h
        if w2_ref is not None:
            o2_ref[...] = jnp.dot(h, w2_ref[...], preferred_element_type=F32)

    o_ref[...] = jnp.dot(h_scr[...], w_ref[...], preferred_element_type=F32).astype(o_ref.dtype)


def norm_matmul(x, g, w, out_dtype, w_side=None, tm=1024, tn=1024):
    t, d = x.shape
    n = w.shape[1]
    tm = _tile(t, tm)
    tn = _tile(n, tn)
    in_specs = [pl.BlockSpec((tm, d), lambda i, j: (i, 0)),
                pl.BlockSpec((1, d), lambda i, j: (0, 0)),
                pl.BlockSpec((d, tn), lambda i, j: (0, j))]
    out_specs = [pl.BlockSpec((tm, tn), lambda i, j: (i, j))]
    out_shape = [jax.ShapeDtypeStruct((t, n), out_dtype)]
    args = [x, g.reshape(1, d).astype(F32), w]
    if w_side is not None:
        ns = w_side.shape[1]
        in_specs.append(pl.BlockSpec((d, ns), lambda i, j: (0, 0)))
        out_specs.append(pl.BlockSpec((tm, ns), lambda i, j: (i, 0)))
        out_shape.append(jax.ShapeDtypeStruct((t, ns), F32))
        args.append(w_side)
    out = pl.pallas_call(
        _norm_mm_kernel,
        grid=(t // tm, n // tn),
        in_specs=in_specs,
        out_specs=out_specs,
        out_shape=out_shape,
        scratch_shapes=[pltpu.VMEM((tm, d), BF16)],
        compiler_params=_cparams("parallel", "arbitrary"),
        name="norm_matmul",
    )(*args)
    return out[0] if w_side is None else out


def _mm_res_kernel(*refs, n_a):
    a_refs = refs[:n_a]
    w_refs = refs[n_a:2 * n_a]
    r_ref, o_ref = refs[2 * n_a], refs[2 * n_a + 1]
    acc = r_ref[...].astype(F32)
    for a_ref, w_ref in zip(a_refs, w_refs):
        acc = acc + jnp.dot(a_ref[...], w_ref[...], preferred_element_type=F32)
    o_ref[...] = acc


def matmul_residual(a_list, w_list, r, tm=1024, tn=1024):
    t, n = r.shape
    tm = _tile(t, tm)
    tn = _tile(n, tn)
    n_a = len(a_list)
    in_specs = [pl.BlockSpec((tm, a.shape[1]), lambda i, j: (i, 0)) for a in a_list]
    in_specs += [pl.BlockSpec((w.shape[0], tn), lambda i, j: (0, j)) for w in w_list]
    in_specs += [pl.BlockSpec((tm, tn), lambda i, j: (i, j))]
    return pl.pallas_call(
        functools.partial(_mm_res_kernel, n_a=n_a),
        grid=(t // tm, n // tn),
        in_specs=in_specs,
        out_specs=pl.BlockSpec((tm, tn), lambda i, j: (i, j)),
        out_shape=jax.ShapeDtypeStruct((t, n), F32),
        compiler_params=_cparams("parallel", "arbitrary"),
        name="matmul_residual",
    )(*a_list, *w_list, r)


def _pair_expand(v, h0, p):
    q = v.shape[0]
    lane = lax.broadcasted_iota(I32, (q, 2 * p), 1)
    return jnp.where(lane < p, v[:, h0:h0 + 1], v[:, h0 + 1:h0 + 2])


def _ssd_core(xs, bm, cm, dtr, bias, alog, state_ref, *, reverse, heads, hdim, groups, nstate):
    q = xs.shape[0]
    per_group = heads // groups
    assert per_group % 2 == 0 and 2 * hdim == LANES
    dt = jax.nn.softplus(dtr + bias)
    dta = dt * (-jnp.exp(alog))
    row = lax.broadcasted_iota(I32, (q, q), 0)
    col = lax.broadcasted_iota(I32, (q, q), 1)
    mask = (row <= col) if reverse else (row >= col)
    tri = jnp.where(mask, 1.0, 0.0).astype(BF16)
    cs = jnp.zeros_like(dta)
    rest = dta
    for _ in range(3):
        piece = rest.astype(BF16)
        cs = cs + jnp.dot(tri, piece, preferred_element_type=F32)
        rest = rest - piece.astype(F32)
    tot = cs[0:1, :] if reverse else cs[q - 1:q, :]
    cs_t = cs.T
    e_out = jnp.exp(cs)
    e_in = jnp.exp(tot - cs)
    e_tot = jnp.exp(tot)
    lane_lo = lax.broadcasted_iota(I32, (q, 2 * hdim), 1) < hdim
    ys = []
    for g in range(groups):
        b_g = bm[:, g * nstate:(g + 1) * nstate]
        c_g = cm[:, g * nstate:(g + 1) * nstate].astype(BF16)
        cb = lax.dot_general(c_g, b_g.astype(BF16), (((1,), (1,)), ((), ())),
                             preferred_element_type=F32)
        st = state_ref[g]
        y_off = jnp.dot(c_g, st.astype(BF16), preferred_element_type=F32)
        x_in, dec = [], []
        for pr in range(per_group // 2):
            h0 = g * per_group + 2 * pr
            c0 = h0 * hdim
            xdt = xs[:, c0:c0 + 2 * hdim] * _pair_expand(dt, h0, hdim)
            xdt16 = xdt.astype(BF16)
            y_pair = []
            for k in range(2):
                h = h0 + k
                seg = jnp.exp(jnp.where(mask, cs[:, h:h + 1] - cs_t[h:h + 1, :], -jnp.inf))
                m = (cb * seg).astype(BF16)
                y_pair.append(jnp.dot(m, xdt16, preferred_element_type=F32))
            y_d = jnp.where(lane_lo, y_pair[0], y_pair[1])
            o0 = 2 * pr * hdim
            ys.append(y_d + y_off[:, o0:o0 + 2 * hdim] * _pair_expand(e_out, h0, hdim))
            x_in.append((xdt * _pair_expand(e_in, h0, hdim)).astype(BF16))
            dec.append(_pair_expand(e_tot, h0, hdim))
        x_in = jnp.concatenate(x_in, axis=1) if len(x_in) > 1 else x_in[0]
        dec = jnp.concatenate(dec, axis=1) if len(dec) > 1 else dec[0]
        state_ref[g] = st * dec + jnp.dot(b_g.T.astype(BF16), x_in, preferred_element_type=F32)
    return jnp.concatenate(ys, axis=1)


def _ssd_fwd_kernel(xc_ref, xp_ref, xn_ref, dt_ref, cw_ref, cb_ref, bias_ref, alog_ref,
                    yf_ref, xact_ref, state_ref, *, dims):
    c = pl.program_id(1)
    nc = pl.num_programs(1)

    @pl.when(c == 0)
    def _():
        state_ref[...] = jnp.zeros_like(state_ref)

    x = xc_ref[...].astype(F32)
    q = x.shape[0]
    prev_row = jnp.where(c > 0, xp_ref[HALO_ROWS - 1:HALO_ROWS, :].astype(F32), 0.0)
    next_row = jnp.where(c < nc - 1, xn_ref[0:1, :].astype(F32), 0.0)
    rows = lax.broadcasted_iota(I32, (q, 1), 0)
    x_dn = jnp.where(rows == 0, prev_row, pltpu.roll(x, 1, 0))
    x_up = jnp.where(rows == q - 1, next_row, pltpu.roll(x, q - 1, 0))
    xc = cw_ref[0:1, :] * x_dn + cw_ref[1:2, :] * x + cw_ref[2:3, :] * x_up + cb_ref[...]
    xa = xc * jax.nn.sigmoid(xc)
    xact_ref[...] = xa.astype(BF16)
    w = dims["heads"] * dims["hdim"]
    gn = dims["groups"] * dims["nstate"]
    y = _ssd_core(xa[:, :w], xa[:, w:w + gn], xa[:, w + gn:], dt_ref[...], bias_ref[...],
                  alog_ref[...], state_ref, reverse=False, **dims)
    yf_ref[...] = y.astype(yf_ref.dtype)


def _ssd_bwd_kernel(xa_ref, dt_ref, yf_ref, z_ref, bias_ref, alog_ref, dsk_ref, ng_ref,
                    y_ref, state_ref, *, dims):
    @pl.when(pl.program_id(1) == 0)
    def _():
        state_ref[...] = jnp.zeros_like(state_ref)

    xa = xa_ref[...].astype(F32)
    w = dims["heads"] * dims["hdim"]
    gn = dims["groups"] * dims["nstate"]
    xs = xa[:, :w]
    yb = _ssd_core(xs, xa[:, w:w + gn], xa[:, w + gn:], dt_ref[...], bias_ref[...],
                   alog_ref[...], state_ref, reverse=True, **dims)
    y = yf_ref[...].astype(F32) + yb + dsk_ref[...] * xs
    z = z_ref[...].astype(F32)
    y = y * (z * jax.nn.sigmoid(z))
    gw = w // dims["groups"]
    parts = []
    for g in range(dims["groups"]):
        yg = y[:, g * gw:(g + 1) * gw]
        parts.append(yg * lax.rsqrt(jnp.mean(yg * yg, axis=-1, keepdims=True) + EPS))
    y_ref[...] = (jnp.concatenate(parts, axis=1) * ng_ref[...]).astype(y_ref.dtype)


def _pad_lanes(v):
    v = v.astype(F32)
    return jnp.pad(v, [(0, 0)] * (v.ndim - 1) + [(0, LANES - v.shape[-1])])


def ssd_mixer(zxu, dt_raw, batch, seq, conv_w, conv_b, dt_bias, a_log, d_skip, norm_g, dims):
    t = zxu.shape[0]
    w = dims["heads"] * dims["hdim"]
    xbc_w = conv_w.shape[-1]
    assert xbc_w % w == 0
    q = min(SSD_CHUNK, seq)
    nc = seq // q
    hb = q // HALO_ROWS
    nhalo = t // HALO_ROWS
    state_shape = (dims["groups"], dims["nstate"], (dims["heads"] // dims["groups"]) * dims["hdim"])
    bias = _pad_lanes(dt_bias)
    alog = _pad_lanes(a_log)
    full = lambda n: pl.BlockSpec((1, n), lambda b, c: (0, 0))
    vec = full(LANES)

    yf, xact = pl.pallas_call(
        functools.partial(_ssd_fwd_kernel, dims=dims),
        grid=(batch, nc),
        in_specs=[pl.BlockSpec((q, xbc_w), lambda b, c: (b * nc + c, 0)),
                  pl.BlockSpec((HALO_ROWS, xbc_w), lambda b, c: (jnp.maximum((b * nc + c) * hb - 1, 0), 0)),
                  pl.BlockSpec((HALO_ROWS, xbc_w), lambda b, c: (jnp.minimum((b * nc + c + 1) * hb, nhalo - 1), 0)),
                  pl.BlockSpec((q, LANES), lambda b, c: (b * nc + c, 0)),
                  pl.BlockSpec((3, xbc_w), lambda b, c: (0, 0)),
                  full(xbc_w), vec, vec],
        out_specs=[pl.BlockSpec((q, w), lambda b, c: (b * nc + c, 0)),
                   pl.BlockSpec((q, xbc_w), lambda b, c: (b * nc + c, 0))],
        out_shape=[jax.ShapeDtypeStruct((t, w), BF16), jax.ShapeDtypeStruct((t, xbc_w), BF16)],
        scratch_shapes=[pltpu.VMEM(state_shape, F32)],
        compiler_params=_cparams("parallel", "arbitrary"),
        name="ssd_forward",
    )(zxu, zxu, zxu, dt_raw, conv_w.astype(F32), conv_b.reshape(1, xbc_w).astype(F32), bias[0:1], alog[0:1])

    rev = lambda b, c: (b * nc + nc - 1 - c, 0)
    z_blk = xbc_w // w
    return pl.pallas_call(
        functools.partial(_ssd_bwd_kernel, dims=dims),
        grid=(batch, nc),
        in_specs=[pl.BlockSpec((q, xbc_w), rev),
                  pl.BlockSpec((q, LANES), rev),
                  pl.BlockSpec((q, w), rev),
                  pl.BlockSpec((q, w), lambda b, c: (b * nc + nc - 1 - c, z_blk)),
                  vec, vec, full(w), full(w)],
        out_specs=pl.BlockSpec((q, w), rev),
        out_shape=jax.ShapeDtypeStruct((t, w), BF16),
        scratch_shapes=[pltpu.VMEM(state_shape, F32)],
        compiler_params=_cparams("parallel", "arbitrary"),
        name="ssd_backward",
    )(xact, dt_raw, yf, zxu, bias[1:2], alog[1:2],
      jnp.repeat(d_skip.astype(F32), dims["hdim"]).reshape(1, w), norm_g.reshape(1, w).astype(F32))


def _s5_tables(a_re, a_im, log_step, b_re, b_im, c_re, c_im):
    qs = S5_CHUNK
    a_re, a_im, b_re, b_im, c_re, c_im = (v.astype(F32) for v in (a_re, a_im, b_re, b_im, c_re, c_im))
    _, g, p = a_re.shape
    cg = b_re.shape[-1]
    delta = jnp.exp(log_step.astype(F32))[..., None]
    mag = jnp.exp(a_re * delta)
    ar = mag * jnp.cos(a_im * delta)
    ai = mag * jnp.sin(a_im * delta)
    den = a_re * a_re + a_im * a_im
    qr = ((ar - 1.0) * a_re + ai * a_im) / den
    qi = (ai * a_re - (ar - 1.0) * a_im) / den
    bbr = qr[..., None] * b_re - qi[..., None] * b_im
    bbi = qr[..., None] * b_im + qi[..., None] * b_re
    pr, pi = [jnp.ones_like(ar)], [jnp.zeros_like(ar)]
    for _ in range(qs):
        pr.append(pr[-1] * ar - pi[-1] * ai)
        pi.append(pr[-2] * ai + pi[-1] * ar)
    pr = jnp.stack(pr)
    pi = jnp.stack(pi)
    cpr = c_re[None] * pr[:, :, :, None, :] - c_im[None] * pi[:, :, :, None, :]
    cpi = c_re[None] * pi[:, :, :, None, :] + c_im[None] * pr[:, :, :, None, :]
    wr = pr[..., None] * bbr[None] - pi[..., None] * bbi[None]
    wi = pr[..., None] * bbi[None] + pi[..., None] * bbr[None]
    kern = (jnp.einsum("tdgcp,dgpe->tdgce", cpr[:qs], bbr, precision=HIGHEST)
            - jnp.einsum("tdgcp,dgpe->tdgce", cpi[:qs], bbi, precision=HIGHEST))
    ii = jnp.arange(qs)[:, None]
    jj = jnp.arange(qs)[None, :]
    lag = ii - jj
    t_f = jnp.where((lag >= 0)[:, :, None, None, None], kern[jnp.clip(lag, 0, qs - 1), 0], 0.0)
    t_b = jnp.where((lag <= 0)[:, :, None, None, None], kern[jnp.clip(-lag, 0, qs - 1), 1], 0.0)
    toep = (t_f + t_b).transpose(2, 1, 4, 0, 3).reshape(g, qs * cg, qs * cg)
    def c_rows(d, powers):
        re = cpr[powers, d].transpose(1, 3, 0, 2)
        im = -cpi[powers, d].transpose(1, 3, 0, 2)
        return jnp.concatenate([re, im], axis=1).reshape(g, 2 * p, qs * cg)
    w_out = jnp.concatenate([toep, c_rows(0, jnp.arange(1, qs + 1)), c_rows(1, qs - jnp.arange(qs))], axis=1)
    def b_cols(d, powers):
        re = wr[powers, d].transpose(1, 0, 3, 2)
        im = wi[powers, d].transpose(1, 0, 3, 2)
        return jnp.concatenate([re, im], axis=3).reshape(g, qs * cg, 2 * p)
    w_state = jnp.concatenate([b_cols(0, qs - 1 - jnp.arange(qs)), b_cols(1, jnp.arange(qs))], axis=2)

    def multipliers(d):
        mul = jnp.concatenate([pr[qs, d], pr[qs, d]], axis=-1).reshape(1, g * 2 * p)
        swp = jnp.concatenate([-pi[qs, d], pi[qs, d]], axis=-1).reshape(1, g * 2 * p)
        return mul, swp
    return w_state.astype(BF16), w_out.astype(BF16), multipliers(0) + multipliers(1)


S5_GROUP_BLOCK = 8


def _s5_state_kernel(u_ref, w_ref, vf_ref, vb_ref):
    half = w_ref.shape[2] // 2
    for k in range(u_ref.shape[0]):
        v = jnp.dot(u_ref[k], w_ref[k], preferred_element_type=F32)
        vf_ref[:, k * half:(k + 1) * half] = v[:, :half]
        vb_ref[:, k * half:(k + 1) * half] = v[:, half:]


def _s5_scan_kernel(v_ref, mul_ref, swp_ref, o_ref, s_scr, *, reverse):
    @pl.when(pl.program_id(1) == 0)
    def _():
        s_scr[...] = jnp.zeros_like(s_scr)

    s = s_scr[...]
    mul = mul_ref[...]
    swp = swp_ref[...]
    steps = v_ref.shape[1]
    width = s.shape[1]
    re_lane = (lax.broadcasted_iota(I32, (1, width), 1) % LANES) < (LANES // 2)
    order = range(steps - 1, -1, -1) if reverse else range(steps)
    for k in order:
        o_ref[:, k, :] = s
        partner = jnp.where(re_lane, pltpu.roll(s, width - LANES // 2, 1), pltpu.roll(s, LANES // 2, 1))
        s = mul * s + swp * partner + v_ref[:, k, :]
    s_scr[...] = s


def _s5_out_kernel(u_ref, sf_ref, sb_ref, w_ref, y_ref):
    qc = u_ref.shape[2]
    ns = (w_ref.shape[1] - qc) // 2
    for k in range(u_ref.shape[0]):
        sf = sf_ref[:, k * ns:(k + 1) * ns].astype(BF16)
        sb = sb_ref[:, k * ns:(k + 1) * ns].astype(BF16)
        y = jnp.dot(u_ref[k], w_ref[k, :qc, :], preferred_element_type=F32)
        y = y + jnp.dot(sf, w_ref[k, qc:qc + ns, :], preferred_element_type=F32)
        y = y + jnp.dot(sb, w_ref[k, qc + ns:, :], preferred_element_type=F32)
        y_ref[k] = y.astype(y_ref.dtype)


def _s5_post_kernel(y_ref, u_ref, d_ref, w_ref, g_ref, o_ref):
    y = y_ref[...].astype(F32) + d_ref[...] * u_ref[...].astype(F32)
    y = jax.nn.gelu(y)
    gate = jax.nn.sigmoid(jnp.dot(y.astype(BF16), w_ref[...].astype(BF16), preferred_element_type=F32))
    o_ref[...] = _rms(y * gate, g_ref[...]).astype(o_ref.dtype)


def _group_lane_blocks(n, cg):
    return lax.broadcasted_iota(I32, (n, LANES), 1) // cg


def _s5_split_kernel(u_ref, o_ref, *, groups, cg):
    n = u_ref.shape[0]
    per_vreg = LANES // cg
    blk = _group_lane_blocks(n, cg)
    for v in range(groups // per_vreg):
        rolled = {}
        for j in range(S5_CHUNK):
            src = u_ref[:, j, v * LANES:(v + 1) * LANES].astype(F32)
            for k in range(per_vreg):
                rolled[j, k] = src if k == 0 else pltpu.roll(src, k * cg, 1)
        for gl in range(per_vreg):
            for h in range(S5_CHUNK // per_vreg):
                acc = rolled[h * per_vreg, (-gl) % per_vreg]
                for jj in range(1, per_vreg):
                    acc = jnp.where(blk == jj, rolled[h * per_vreg + jj, (jj - gl) % per_vreg], acc)
                o_ref[v * per_vreg + gl, :, h * LANES:(h + 1) * LANES] = acc.astype(o_ref.dtype)


def _s5_merge_kernel(y_ref, o_ref, *, groups, cg):
    n = y_ref.shape[1]
    per_vreg = LANES // cg
    blk = _group_lane_blocks(n, cg)
    for v in range(groups // per_vreg):
        rolled = {}
        for gl in range(per_vreg):
            for h in range(S5_CHUNK // per_vreg):
                src = y_ref[v * per_vreg + gl, :, h * LANES:(h + 1) * LANES].astype(F32)
                for k in range(per_vreg):
                    rolled[gl, h, k] = src if k == 0 else pltpu.roll(src, k * cg, 1)
        for i in range(S5_CHUNK):
            h, ii = divmod(i, per_vreg)
            acc = rolled[0, h, (-ii) % per_vreg]
            for gl in range(1, per_vreg):
                acc = jnp.where(blk == gl, rolled[gl, h, (gl - ii) % per_vreg], acc)
            o_ref[:, i, v * LANES:(v + 1) * LANES] = acc.astype(o_ref.dtype)


def s5_mixer(zxu, u_blk, batch, seq, tables, d_skip, w_glu, norm_g):
    w_state, w_out, (mul_f, swp_f, mul_b, swp_b) = tables
    t = zxu.shape[0]
    g, qc, ns2 = w_state.shape
    ns = ns2 // 2
    cg = qc // S5_CHUNK
    width = g * cg
    nchunk = seq // S5_CHUNK
    rows = nchunk * batch
    u_rows = zxu.reshape(rows, S5_CHUNK, zxu.shape[1])
    rn = min(64, rows)
    ug = pl.pallas_call(
        functools.partial(_s5_split_kernel, groups=g, cg=cg),
        grid=(rows // rn,),
        in_specs=[pl.BlockSpec((rn, S5_CHUNK, width), lambda r: (r, 0, u_blk))],
        out_specs=pl.BlockSpec((g, rn, qc), lambda r: (0, r, 0)),
        out_shape=jax.ShapeDtypeStruct((g, rows, qc), BF16),
        compiler_params=_cparams("parallel"),
        name="s5_split",
    )(u_rows)
    tr = _tile(rows, 1024)
    gb = min(S5_GROUP_BLOCK, g)
    vf, vb = pl.pallas_call(
        _s5_state_kernel,
        grid=(g // gb, rows // tr),
        in_specs=[pl.BlockSpec((gb, tr, qc), lambda i, r: (i, r, 0)),
                  pl.BlockSpec((gb, qc, ns2), lambda i, r: (i, 0, 0))],
        out_specs=[pl.BlockSpec((tr, gb * ns), lambda i, r: (r, i))] * 2,
        out_shape=[jax.ShapeDtypeStruct((rows, g * ns), F32)] * 2,
        compiler_params=_cparams("parallel", "parallel"),
        name="s5_state_in",
    )(ug, w_state)

    panel = _tile(g * ns, 4096)
    steps = min(32, nchunk)
    nblk = nchunk // steps

    def scan(v, mul, swp, reverse):
        rmap = (lambda pnl, i: (0, nblk - 1 - i, pnl)) if reverse else (lambda pnl, i: (0, i, pnl))
        return pl.pallas_call(
            functools.partial(_s5_scan_kernel, reverse=reverse),
            grid=(g * ns // panel, nblk),
            in_specs=[pl.BlockSpec((batch, steps, panel), rmap),
                      pl.BlockSpec((1, panel), lambda pnl, i: (0, pnl)),
                      pl.BlockSpec((1, panel), lambda pnl, i: (0, pnl))],
            out_specs=pl.BlockSpec((batch, steps, panel), rmap),
            out_shape=jax.ShapeDtypeStruct((batch, nchunk, g * ns), F32),
            scratch_shapes=[pltpu.VMEM((batch, panel), F32)],
            compiler_params=_cparams("parallel", "arbitrary"),
            name="s5_scan_bwd" if reverse else "s5_scan_fwd",
        )(v.reshape(batch, nchunk, g * ns), mul, swp).reshape(rows, g * ns)

    sf = scan(vf, mul_f, swp_f, False)
    sb = scan(vb, mul_b, swp_b, True)
    yg = pl.pallas_call(
        _s5_out_kernel,
        grid=(g // gb, rows // tr),
        in_specs=[pl.BlockSpec((gb, tr, qc), lambda i, r: (i, r, 0)),
                  pl.BlockSpec((tr, gb * ns), lambda i, r: (r, i)),
                  pl.BlockSpec((tr, gb * ns), lambda i, r: (r, i)),
                  pl.BlockSpec((gb, qc + 2 * ns, qc), lambda i, r: (i, 0, 0))],
        out_specs=pl.BlockSpec((gb, tr, qc), lambda i, r: (i, r, 0)),
        out_shape=jax.ShapeDtypeStruct((g, rows, qc), BF16),
        compiler_params=_cparams("parallel", "parallel"),
        name="s5_state_out",
    )(ug, sf, sb, w_out)
    y_rows = pl.pallas_call(
        functools.partial(_s5_merge_kernel, groups=g, cg=cg),
        grid=(rows // rn,),
        in_specs=[pl.BlockSpec((g, rn, qc), lambda r: (0, r, 0))],
        out_specs=pl.BlockSpec((rn, S5_CHUNK, width), lambda r: (r, 0, 0)),
        out_shape=jax.ShapeDtypeStruct((rows, S5_CHUNK, width), F32),
        compiler_params=_cparams("parallel"),
        name="s5_merge",
    )(yg)
    y = y_rows.reshape(t, width)
    tm = _tile(t, 1024)
    return pl.pallas_call(
        _s5_post_kernel,
        grid=(t // tm,),
        in_specs=[pl.BlockSpec((tm, width), lambda i: (i, 0)),
                  pl.BlockSpec((tm, width), lambda i: (i, u_blk)),
                  pl.BlockSpec((1, width), lambda i: (0, 0)),
                  pl.BlockSpec((width, width), lambda i: (0, 0)),
                  pl.BlockSpec((1, width), lambda i: (0, 0))],
        out_specs=pl.BlockSpec((tm, width), lambda i: (i, 0)),
        out_shape=jax.ShapeDtypeStruct((t, width), BF16),
        compiler_params=_cparams("parallel"),
        name="s5_post",
    )(y, zxu, d_skip.reshape(1, width).astype(F32), w_glu.astype(F32), norm_g.reshape(1, width).astype(F32))


def _attn_kernel(q_ref, kv_ref, o_ref, *, heads):
    d = q_ref.shape[1]
    hd = d // heads
    scale = hd ** -0.5
    for h in range(heads):
        qh = q_ref[:, h * hd:(h + 1) * hd]
        kh = kv_ref[:, h * hd:(h + 1) * hd]
        vh = kv_ref[:, d + h * hd:d + (h + 1) * hd]
        s = lax.dot_general(qh, kh, (((1,), (1,)), ((), ())), preferred_element_type=F32) * scale
        p = jnp.exp(s - jnp.max(s, axis=-1, keepdims=True))
        o = jnp.dot(p.astype(BF16), vh, preferred_element_type=F32) / jnp.sum(p, axis=-1, keepdims=True)
        o_ref[:, h * hd:(h + 1) * hd] = o.astype(o_ref.dtype)


def mem_attention(q, kv, batch, seq, mem_len):
    t, d = q.shape
    tq = _tile(seq, 1024)
    nq = seq // tq
    return pl.pallas_call(
        functools.partial(_attn_kernel, heads=MEM_HEADS),
        grid=(batch, nq),
        in_specs=[pl.BlockSpec((tq, d), lambda b, i: (b * nq + i, 0)),
                  pl.BlockSpec((mem_len, 2 * d), lambda b, i: (b, 0))],
        out_specs=pl.BlockSpec((tq, d), lambda b, i: (b * nq + i, 0)),
        out_shape=jax.ShapeDtypeStruct((t, d), BF16),
        compiler_params=_cparams("parallel", "arbitrary"),
        name="mem_attention",
    )(q, kv)


def _router_kernel(x_ref, g_ref, wr_ref, hn_ref, p_ref):
    hn = _rms(x_ref[...], g_ref[...])
    hn_hi = hn.astype(BF16)
    hn_ref[...] = hn_hi
    hn_lo = (hn - hn_hi.astype(F32)).astype(BF16)
    w = wr_ref[...]
    w_hi = w.astype(BF16)
    w_lo = (w - w_hi.astype(F32)).astype(BF16)
    nt = lambda a, b: lax.dot_general(a, b, (((1,), (1,)), ((), ())), preferred_element_type=F32)
    logits = nt(w_hi, hn_hi) + nt(w_lo, hn_hi) + nt(w_hi, hn_lo)
    e = jnp.exp(logits - jnp.max(logits, axis=0, keepdims=True))
    p_ref[...] = e / jnp.sum(e, axis=0, keepdims=True)


def router(x, g, w_router, tm=1024):
    t, d = x.shape
    e = w_router.shape[1]
    tm = _tile(t, tm)
    return pl.pallas_call(
        _router_kernel,
        grid=(t // tm,),
        in_specs=[pl.BlockSpec((tm, d), lambda i: (i, 0)),
                  pl.BlockSpec((1, d), lambda i: (0, 0)),
                  pl.BlockSpec((e, d), lambda i: (0, 0))],
        out_specs=[pl.BlockSpec((tm, d), lambda i: (i, 0)),
                   pl.BlockSpec((e, tm), lambda i: (0, i))],
        out_shape=[jax.ShapeDtypeStruct((t, d), BF16), jax.ShapeDtypeStruct((e, t), F32)],
        compiler_params=_cparams("parallel"),
        name="router",
    )(x, g.reshape(1, d).astype(F32), w_router.T.astype(F32))


def _select_kernel(p_ref, pos_ref, gate_ref, cnt_ref, *, cap, tb):
    e, t = p_ref.shape
    bits = pltpu.bitcast(p_ref[...], I32)

    def search(i, prefix):
        cand = prefix | jnp.left_shift(jnp.int32(1), 30 - i)
        cnt = jnp.sum((bits >= cand).astype(I32), axis=1, keepdims=True)
        return jnp.where(cnt >= cap, cand, prefix)

    thr = lax.fori_loop(0, 31, search, jnp.zeros((e, 1), I32))
    need = (cap - jnp.sum((bits > thr).astype(I32), axis=1, keepdims=True)).astype(F32)
    r = lax.broadcasted_iota(I32, (tb, tb), 0)
    c = lax.broadcasted_iota(I32, (tb, tb), 1)
    incl = (r <= c).astype(BF16)
    strict = (r < c).astype(BF16)

    def block(k, carry):
        carry_eq, carry_sel = carry
        sl = pl.ds(pl.multiple_of(k * tb, tb), tb)
        p = p_ref[:, sl]
        b = pltpu.bitcast(p, I32)
        eq = b == thr
        eq16 = jnp.where(eq, 1.0, 0.0).astype(BF16)
        rank = jnp.dot(eq16, incl, preferred_element_type=F32) + carry_eq
        sel = (b > thr) | (eq & (rank <= need))
        sel16 = jnp.where(sel, 1.0, 0.0).astype(BF16)
        pos = jnp.dot(sel16, strict, preferred_element_type=F32) + carry_sel
        pos_ref[:, sl] = jnp.where(sel, pos, -1.0)
        gate_ref[:, sl] = jnp.where(sel, p, 0.0)
        cnt_ref[k] = jnp.broadcast_to(carry_sel, (e, LANES))
        return (carry_eq + jnp.sum(eq16.astype(F32), axis=1, keepdims=True),
                carry_sel + jnp.sum(sel16.astype(F32), axis=1, keepdims=True))

    lax.fori_loop(0, t // tb, block, (jnp.zeros((e, 1), F32), jnp.zeros((e, 1), F32)))


def select_topk(probs_t, cap, tb):
    e, t = probs_t.shape
    nb = t // tb
    return pl.pallas_call(
        functools.partial(_select_kernel, cap=cap, tb=tb),
        grid=(1,),
        in_specs=[pl.BlockSpec((e, t), lambda i: (0, 0))],
        out_specs=[pl.BlockSpec((e, t), lambda i: (0, 0)),
                   pl.BlockSpec((e, t), lambda i: (0, 0)),
                   pl.BlockSpec((nb, e, LANES), lambda i: (0, 0, 0))],
        out_shape=[jax.ShapeDtypeStruct((e, t), F32), jax.ShapeDtypeStruct((e, t), F32),
                   jax.ShapeDtypeStruct((nb, e, LANES), F32)],
        compiler_params=_cparams("arbitrary"),
        name="select_topk",
    )(probs_t)


GATHER_WINDOW = 192


def _gather_kernel(a0_ref, nw_ref, pos_ref, hn_ref, xe_ref, *, cap, win):
    e = pl.program_id(0)
    kb = pl.program_id(1)

    @pl.when(kb == 0)
    def _():
        xe_ref[...] = jnp.zeros_like(xe_ref)

    item = e * pl.num_programs(1) + kb
    tb = hn_ref.shape[0]
    slot = pos_ref[pl.ds(e, 1), :]
    local = lax.broadcasted_iota(I32, (win, tb), 0).astype(F32)

    def window(w, carry):
        start = a0_ref[item] * ROW_ALIGN + w * win
        base = jnp.minimum(a0_ref[item] + w * (win // ROW_ALIGN), (cap - win) // ROW_ALIGN) * ROW_ALIGN
        hit = (slot - base.astype(F32) == local) & (slot >= start.astype(F32))
        rows = jnp.dot(jnp.where(hit, 1.0, 0.0).astype(BF16), hn_ref[...], preferred_element_type=F32)
        sl = pl.ds(pl.multiple_of(base, ROW_ALIGN), win)
        xe_ref[sl, :] = xe_ref[sl, :] + rows.astype(xe_ref.dtype)
        return carry

    lax.fori_loop(0, nw_ref[item], window, 0)


def moe_gather(hn, pos_t, bounds, cap, tb):
    t, d = hn.shape
    e = pos_t.shape[0]
    win = min(GATHER_WINDOW, cap)
    assert cap % ROW_ALIGN == 0 and win % ROW_ALIGN == 0
    c0, c1 = bounds[:, :-1], bounds[:, 1:]
    a0 = c0 // ROW_ALIGN
    nw = jnp.where(c1 > c0, (c1 - a0 * ROW_ALIGN + win - 1) // win, 0)
    grid_spec = pltpu.PrefetchScalarGridSpec(
        num_scalar_prefetch=2,
        grid=(e, t // tb),
        in_specs=[pl.BlockSpec((e, tb), lambda x, k, a0, nw: (0, k)),
                  pl.BlockSpec((tb, d), lambda x, k, a0, nw: (k, 0))],
        out_specs=pl.BlockSpec((cap, d), lambda x, k, a0, nw: (x, 0)),
    )
    return pl.pallas_call(
        functools.partial(_gather_kernel, cap=cap, win=win),
        grid_spec=grid_spec,
        out_shape=jax.ShapeDtypeStruct((e * cap, d), BF16),
        compiler_params=_cparams("parallel", "arbitrary"),
        name="moe_gather",
    )(a0.reshape(-1).astype(I32), nw.reshape(-1).astype(I32), pos_t, hn)


def _ffn_kernel(xe_ref, wg_ref, wu_ref, wd_ref, ye_ref, acc_ref):
    f = pl.program_id(2)

    @pl.when(f == 0)
    def _():
        acc_ref[...] = jnp.zeros_like(acc_ref)

    xe = xe_ref[...]
    gate = jnp.dot(xe, wg_ref[0].astype(BF16), preferred_element_type=F32)
    up = jnp.dot(xe, wu_ref[0].astype(BF16), preferred_element_type=F32)
    he = (gate * jax.nn.sigmoid(gate) * up).astype(BF16)
    acc_ref[...] += jnp.dot(he, wd_ref[0].astype(BF16), preferred_element_type=F32)

    @pl.when(f == pl.num_programs(2) - 1)
    def _():
        ye_ref[...] = acc_ref[...].astype(ye_ref.dtype)


def moe_ffn(xe, w_gate, w_up, w_down, cap, ts=1024, tf=256):
    n, d = xe.shape
    e, _, ff = w_gate.shape
    ts = _tile(cap, ts)
    tf = _tile(ff, tf)
    nsb = cap // ts
    return pl.pallas_call(
        _ffn_kernel,
        grid=(e, nsb, ff // tf),
        in_specs=[pl.BlockSpec((ts, d), lambda x, s, f: (x * nsb + s, 0)),
                  pl.BlockSpec((1, d, tf), lambda x, s, f: (x, 0, f)),
                  pl.BlockSpec((1, d, tf), lambda x, s, f: (x, 0, f)),
                  pl.BlockSpec((1, tf, d), lambda x, s, f: (x, f, 0))],
        out_specs=pl.BlockSpec((ts, d), lambda x, s, f: (x * nsb + s, 0)),
        out_shape=jax.ShapeDtypeStruct((n, d), BF16),
        scratch_shapes=[pltpu.VMEM((ts, d), F32)],
        compiler_params=_cparams("parallel", "parallel", "arbitrary"),
        name="moe_ffn",
    )(xe, w_gate, w_up, w_down)


SLOT_SPLIT = 64
SCATTER_WINDOW = 128
SCATTER_WINDOWS = 8
ROW_ALIGN = 16
NO_SLOT = 1 << 24


def _scatter_kernel(kb_ref, first_ref, last_ref, real_ref, ex_ref, off_ref, base_ref, vlo_ref,
                    tok_ref, *rest, experts, final_norm):
    ye_refs = rest[:SCATTER_WINDOWS]
    x_ref, g_ref, y_ref, acc_ref = rest[SCATTER_WINDOWS:]
    i = pl.program_id(0)
    win = SCATTER_WINDOW

    @pl.when(first_ref[i] == 1)
    def _():
        acc_ref[...] = jnp.zeros_like(acc_ref)

    @pl.when(real_ref[i] == 1)
    def _():
        tok = tok_ref[...]
        kdim = tok.shape[1]
        e = experts
        row = lax.broadcasted_iota(I32, (kdim, 2 * win), 0)
        col = lax.broadcasted_iota(I32, (kdim, 2 * win), 1)
        lane = lax.broadcasted_iota(I32, (1, win), 1).astype(F32)
        pieces = []
        for k in range(SCATTER_WINDOWS):
            w = i * SCATTER_WINDOWS + k
            ek = ex_ref[w]
            sel = jnp.where((row == ek) & (col < win), float(SLOT_SPLIT), 0.0)
            sel = sel + jnp.where((row == e + ek) & (col < win), 1.0, 0.0)
            sel = sel + jnp.where((row == 2 * e + ek) & (col >= win), 1.0, 0.0)
            both = jnp.dot(tok, sel.astype(BF16), preferred_element_type=F32)
            slot = both[:, :win]
            hit = (slot - base_ref[w].astype(F32) == lane) & (slot >= vlo_ref[w].astype(F32))
            pieces.append(jnp.where(hit, both[:, win:], 0.0).astype(BF16))
        onehot = jnp.concatenate(pieces, axis=1)
        rows = jnp.concatenate([r[...] for r in ye_refs], axis=0)
        acc_ref[...] += jnp.dot(onehot, rows, preferred_element_type=F32)

    @pl.when(last_ref[i] == 1)
    def _():
        y = x_ref[...] + acc_ref[...]
        y_ref[...] = _rms(y, g_ref[...]) if final_norm else y


def _scatter_items(bounds, cap, tb):
    e, nkb1 = bounds.shape
    nkb = nkb1 - 1
    win, grp = SCATTER_WINDOW, SCATTER_WINDOWS
    c0, c1 = bounds[:, :-1].T, bounds[:, 1:].T
    a0 = (c0 // ROW_ALIGN) * ROW_ALIGN
    nwin = jnp.where(c1 > c0, (c1 - a0 + win - 1) // win, 0)
    wmax = (tb + ROW_ALIGN - 1 + win - 1) // win
    w = jnp.arange(wmax, dtype=I32)
    valid = (w[None, None, :] < nwin[:, :, None]).reshape(-1)
    start = (a0[:, :, None] + w * win).reshape(-1)
    base = jnp.minimum(start, cap - win)
    expert = jnp.broadcast_to(jnp.arange(e, dtype=I32)[None, :, None], (nkb, e, wmax)).reshape(-1)
    block = jnp.broadcast_to(jnp.arange(nkb, dtype=I32)[:, None, None], (nkb, e, wmax)).reshape(-1)
    per_block = jnp.sum(valid.reshape(nkb, -1).astype(I32), axis=1)
    items_per_block = jnp.maximum((per_block + grp - 1) // grp, 1)
    item_end = jnp.cumsum(items_per_block)
    item_start = item_end - items_per_block
    rank = jnp.cumsum(valid.astype(I32)) - 1 - (jnp.cumsum(per_block) - per_block)[block]
    max_windows = e * cap // win + (e * nkb * (win + ROW_ALIGN - 2) + win - 1) // win
    max_items = (max_windows + grp - 1) // grp + nkb
    dest = jnp.where(valid, (item_start[block] + rank // grp) * grp + rank % grp, max_items * grp)

    fill = jnp.broadcast_to(jnp.array([0, 0, NO_SLOT, 0], I32), (max_items * grp, 4))
    values = jnp.stack([expert, (expert * cap + base) // ROW_ALIGN, base, start], axis=1).astype(I32)
    placed = fill.at[dest].set(values, mode="drop")
    it = jnp.arange(max_items, dtype=I32)
    real = it < item_end[-1]
    kb = jnp.minimum(jnp.sum((it[:, None] >= item_end[None, :]).astype(I32), axis=1), nkb - 1)
    first = real & (it == item_start[kb])
    last = real & (it == item_end[kb] - 1)
    return (kb, first.astype(I32), last.astype(I32), real.astype(I32),
            placed[:, 0], placed[:, 1], placed[:, 2], placed[:, 3])


def moe_scatter_norm(ye, pos_t, gate_t, x, g_final, final_norm, bounds, cap, tb):
    t, d = x.shape
    e = pos_t.shape[0]
    assert 3 * e <= LANES and cap % SCATTER_WINDOW == 0
    pos_c = pos_t.T
    hi = jnp.floor(pos_c / SLOT_SPLIT)
    tok = jnp.concatenate([hi, pos_c - SLOT_SPLIT * hi, gate_t.T, jnp.zeros((t, LANES - 3 * e), F32)],
                          axis=1).astype(BF16)
    items = _scatter_items(bounds, cap, tb)
    grp = SCATTER_WINDOWS
    tokmap = lambda i, kb, *_: (kb[i], 0)

    def window(k):
        return pl.BlockSpec((pl.Element(SCATTER_WINDOW), pl.Element(d)),
                            lambda i, kb, f, l, r, ex, off, base, vlo: (off[i * grp + k] * ROW_ALIGN, 0))

    grid_spec = pltpu.PrefetchScalarGridSpec(
        num_scalar_prefetch=8,
        grid=(items[0].shape[0],),
        in_specs=[pl.BlockSpec((tb, LANES), tokmap)] + [window(k) for k in range(grp)]
                 + [pl.BlockSpec((tb, d), tokmap), pl.BlockSpec((1, d), lambda i, *_: (0, 0))],
        out_specs=pl.BlockSpec((tb, d), tokmap),
        scratch_shapes=[pltpu.VMEM((tb, d), F32)],
    )
    return pl.pallas_call(
        functools.partial(_scatter_kernel, experts=e, final_norm=final_norm),
        grid_spec=grid_spec,
        out_shape=jax.ShapeDtypeStruct((t, d), F32),
        compiler_params=_cparams("arbitrary"),
        name="moe_scatter_norm",
    )(*items, tok, *([ye] * grp), x, g_final.reshape(1, d).astype(F32))


def expert_choice_moe_norm(x, norm_g, w_router, w_gate, w_up, w_down, g_final, final_norm):
    t, d = x.shape
    e = w_router.shape[1]
    cap = CAPACITY_FACTOR * t // e
    sel_tb = _tile(t, 512)
    hn, probs_t = router(x, norm_g, w_router)
    pos_t, gate_t, cnt = select_topk(probs_t, cap, sel_tb)
    counts = cnt[:, :, 0].T.astype(I32)
    bounds = jnp.concatenate([counts, jnp.full((e, 1), cap, I32)], axis=1)
    g_tb = _tile(t, 1024)
    xe = moe_gather(hn, pos_t, bounds[:, ::g_tb // sel_tb], cap, g_tb)
    ye = moe_ffn(xe, w_gate, w_up, w_down, cap)
    return moe_scatter_norm(ye, pos_t, gate_t, x, g_final, final_norm, bounds, cap, sel_tb)


def _prepare(p, depth_i):
    i = depth_i
    ssd_w = p["ssd_norm"].shape[-1]
    xbc_w = p["conv_w"].shape[-1]
    heads = p["ssd_a_log"].shape[-1]
    w_in = p["w_in"][i]
    o_z, o_x, o_dt = ssd_w, ssd_w + xbc_w, ssd_w + xbc_w + heads
    w_zxu = jnp.concatenate([w_in[:, o_z:o_x], w_in[:, :o_z], w_in[:, o_dt:]], axis=1).astype(BF16)
    w_dt = _pad_lanes(w_in[:, o_x:o_dt]).astype(BF16)
    dims = dict(heads=heads, hdim=ssd_w // heads, groups=SSD_GROUPS,
                nstate=(xbc_w - ssd_w) // (2 * SSD_GROUPS))
    tables = _s5_tables(p["s5_a_re"][i], p["s5_a_im"][i], p["s5_log_step"][i], p["s5_b_re"][i],
                        p["s5_b_im"][i], p["s5_c_re"][i], p["s5_c_im"][i])
    w_out = p["w_out"][i].astype(BF16)
    return dict(
        w_zxu=w_zxu, w_dt=w_dt, dims=dims, tables=tables, u_blk=(xbc_w + ssd_w) // (w_in.shape[1] - o_dt),
        w_out_ssd=w_out[:ssd_w], w_out_s5=w_out[ssd_w:],
        w_q=p["w_q"][i].astype(BF16),
        w_kv=jnp.concatenate([p["w_k"][i], p["w_v"][i]], axis=1).astype(BF16),
        w_o=p["w_o"][i].astype(BF16),
        w_gate=p["w_gate"][i], w_up=p["w_up"][i], w_down=p["w_down"][i],
    )


def _trunk(x, mem, p, prepared):
    batch, seq, d = x.shape
    mem_len = mem.shape[1]
    t = batch * seq
    x = x.reshape(t, d)
    mem = mem.reshape(batch * mem_len, d)
    depth = p["w_in"].shape[0]
    for i in range(depth):
        w = prepared[i]
        zxu, dt_raw = norm_matmul(x, p["norm_mix"][i], w["w_zxu"], BF16, w_side=w["w_dt"])
        y_ssd = ssd_mixer(zxu, dt_raw, batch, seq, p["conv_w"][i], p["conv_b"][i], p["ssd_dt_bias"][i],
                          p["ssd_a_log"][i], p["ssd_d"][i], p["ssd_norm"][i], w["dims"])
        y_s5 = s5_mixer(zxu, w["u_blk"], batch, seq, w["tables"], p["s5_d"][i], p["s5_w_glu"][i],
                        p["s5_norm"][i])
        x = matmul_residual([y_ssd, y_s5], [w["w_out_ssd"], w["w_out_s5"]], x)
        q = norm_matmul(x, p["norm_attn"][i], w["w_q"], BF16)
        kv = norm_matmul(mem, p["norm_mem"][i], w["w_kv"], BF16)
        o = mem_attention(q, kv, batch, seq, mem_len)
        x = matmul_residual([o], [w["w_o"]], x)
        x = expert_choice_moe_norm(x, p["norm_ffn"][i], p["w_router"][i], w["w_gate"], w["w_up"],
                                   w["w_down"], p["norm_final"], i == depth - 1)
    return x.reshape(batch, seq, d)


def kernel(x_prompt, x_sample, mem_prompt, mem_sample, norm_mix, w_in, conv_w, conv_b, ssd_dt_bias,
           ssd_a_log, ssd_d, ssd_norm, s5_a_re, s5_a_im, s5_log_step, s5_b_re, s5_b_im, s5_c_re,
           s5_c_im, s5_d, s5_w_glu, s5_norm, w_out, norm_attn, norm_mem, w_q, w_k, w_v, w_o,
           norm_ffn, w_router, w_gate, w_up, w_down, norm_final):
    p = dict(norm_mix=norm_mix, w_in=w_in, conv_w=conv_w, conv_b=conv_b, ssd_dt_bias=ssd_dt_bias,
             ssd_a_log=ssd_a_log, ssd_d=ssd_d, ssd_norm=ssd_norm, s5_a_re=s5_a_re, s5_a_im=s5_a_im,
             s5_log_step=s5_log_step, s5_b_re=s5_b_re, s5_b_im=s5_b_im, s5_c_re=s5_c_re,
             s5_c_im=s5_c_im, s5_d=s5_d, s5_w_glu=s5_w_glu, s5_norm=s5_norm, w_out=w_out,
             norm_attn=norm_attn, norm_mem=norm_mem, w_q=w_q, w_k=w_k, w_v=w_v, w_o=w_o,
             norm_ffn=norm_ffn, w_router=w_router, w_gate=w_gate, w_up=w_up, w_down=w_down,
             norm_final=norm_final)
    prepared = [_prepare(p, i) for i in range(w_in.shape[0])]
    return (_trunk(x_prompt, mem_prompt, p, prepared), _trunk(x_sample, mem_sample, p, prepared))
```

```python
import functools

import jax
import jax.numpy as jnp
from jax import lax
from jax.experimental import pallas as pl
from jax.experimental.pallas import tpu as pltpu

F32 = jnp.float32
BF16 = jnp.bfloat16
I32 = jnp.int32
HIGHEST = lax.Precision.HIGHEST

EPS = 1e-6
SSD_GROUPS = 4
MEM_HEADS = 4
CAPACITY_FACTOR = 2
LANES = 128
HALO_ROWS = 16
SSD_CHUNK = 256
S5_CHUNK = 16
VMEM_LIMIT = 56 * 1024 * 1024


def _cparams(*sem):
    return pltpu.CompilerParams(dimension_semantics=sem, vmem_limit_bytes=VMEM_LIMIT)


def _tile(n, target):
    if n <= target:
        return n
    t = (target // LANES) * LANES
    while n % t:
        t -= LANES
    return t


def _rms(x, g):
    return x * lax.rsqrt(jnp.mean(x * x, axis=-1, keepdims=True) + EPS) * g


def _norm_mm_kernel(x_ref, g_ref, w_ref, *rest):
    w2_ref = o2_ref = None
    if len(rest) == 2:
        o_ref, h_scr = rest
    else:
        w2_ref, o_ref, o2_ref, h_scr = rest

    @pl.when(pl.program_id(1) == 0)
    def _():
        h = _rms(x_ref[...].astype(F32), g_ref[...]).astype(BF16)
        h_scr[...] = h
        if w2_ref is not None:
            o2_ref[...] = jnp.dot(h, w2_ref[...], preferred_element_type=F32)

    o_ref[...] = jnp.dot(h_scr[...], w_ref[...], preferred_element_type=F32).astype(o_ref.dtype)


def norm_matmul(x, g, w, out_dtype, w_side=None, tm=1024, tn=1024):
    t, d = x.shape
    n = w.shape[1]
    tm = _tile(t, tm)
    tn = _tile(n, tn)
    in_specs = [pl.BlockSpec((tm, d), lambda i, j: (i, 0)),
                pl.BlockSpec((1, d), lambda i, j: (0, 0)),
                pl.BlockSpec((d, tn), lambda i, j: (0, j))]
    out_specs = [pl.BlockSpec((tm, tn), lambda i, j: (i, j))]
    out_shape = [jax.ShapeDtypeStruct((t, n), out_dtype)]
    args = [x, g.reshape(1, d).astype(F32), w]
    if w_side is not None:
        ns = w_side.shape[1]
        in_specs.append(pl.BlockSpec((d, ns), lambda i, j: (0, 0)))
        out_specs.append(pl.BlockSpec((tm, ns), lambda i, j: (i, 0)))
        out_shape.append(jax.ShapeDtypeStruct((t, ns), F32))
        args.append(w_side)
    out = pl.pallas_call(
        _norm_mm_kernel,
        grid=(t // tm, n // tn),
        in_specs=in_specs,
        out_specs=out_specs,
        out_shape=out_shape,
        scratch_shapes=[pltpu.VMEM((tm, d), BF16)],
        compiler_params=_cparams("parallel", "arbitrary"),
        name="norm_matmul",
    )(*args)
    return out[0] if w_side is None else out


def _mm_res_kernel(*refs, n_a):
    a_refs = refs[:n_a]
    w_refs = refs[n_a:2 * n_a]
    r_ref, o_ref = refs[2 * n_a], refs[2 * n_a + 1]
    acc = r_ref[...].astype(F32)
    for a_ref, w_ref in zip(a_refs, w_refs):
        acc = acc + jnp.dot(a_ref[...], w_ref[...], preferred_element_type=F32)
    o_ref[...] = acc


def matmul_residual(a_list, w_list, r, tm=1024, tn=1024):
    t, n = r.shape
    tm = _tile(t, tm)
    tn = _tile(n, tn)
    n_a = len(a_list)
    in_specs = [pl.BlockSpec((tm, a.shape[1]), lambda i, j: (i, 0)) for a in a_list]
    in_specs += [pl.BlockSpec((w.shape[0], tn), lambda i, j: (0, j)) for w in w_list]
    in_specs += [pl.BlockSpec((tm, tn), lambda i, j: (i, j))]
    return pl.pallas_call(
        functools.partial(_mm_res_kernel, n_a=n_a),
        grid=(t // tm, n // tn),
        in_specs=in_specs,
        out_specs=pl.BlockSpec((tm, tn), lambda i, j: (i, j)),
        out_shape=jax.ShapeDtypeStruct((t, n), F32),
        compiler_params=_cparams("parallel", "arbitrary"),
        name="matmul_residual",
    )(*a_list, *w_list, r)


def _pair_expand(v, h0, p):
    q = v.shape[0]
    lane = lax.broadcasted_iota(I32, (q, 2 * p), 1)
    return jnp.where(lane < p, v[:, h0:h0 + 1], v[:, h0 + 1:h0 + 2])


def _ssd_core(xs, bm, cm, dtr, bias, alog, state_ref, *, reverse, heads, hdim, groups, nstate):
    q = xs.shape[0]
    per_group = heads // groups
    assert per_group % 2 == 0 and 2 * hdim == LANES
    dt = jax.nn.softplus(dtr + bias)
    dta = dt * (-jnp.exp(alog))
    row = lax.broadcasted_iota(I32, (q, q), 0)
    col = lax.broadcasted_iota(I32, (q, q), 1)
    mask = (row <= col) if reverse else (row >= col)
    tri = jnp.where(mask, 1.0, 0.0).astype(BF16)
    cs = jnp.zeros_like(dta)
    rest = dta
    for _ in range(3):
        piece = rest.astype(BF16)
        cs = cs + jnp.dot(tri, piece, preferred_element_type=F32)
        rest = rest - piece.astype(F32)
    tot = cs[0:1, :] if reverse else cs[q - 1:q, :]
    cs_t = cs.T
    e_out = jnp.exp(cs)
    e_in = jnp.exp(tot - cs)
    e_tot = jnp.exp(tot)
    lane_lo = lax.broadcasted_iota(I32, (q, 2 * hdim), 1) < hdim
    ys = []
    for g in range(groups):
        b_g = bm[:, g * nstate:(g + 1) * nstate]
        c_g = cm[:, g * nstate:(g + 1) * nstate].astype(BF16)
        cb = lax.dot_general(c_g, b_g.astype(BF16), (((1,), (1,)), ((), ())),
                             preferred_element_type=F32)
        st = state_ref[g]
        y_off = jnp.dot(c_g, st.astype(BF16), preferred_element_type=F32)
        x_in, dec = [], []
        for pr in range(per_group // 2):
            h0 = g * per_group + 2 * pr
            c0 = h0 * hdim
            xdt = xs[:, c0:c0 + 2 * hdim] * _pair_expand(dt, h0, hdim)
            xdt16 = xdt.astype(BF16)
            y_pair = []
            for k in range(2):
                h = h0 + k
                seg = jnp.exp(jnp.where(mask, cs[:, h:h + 1] - cs_t[h:h + 1, :], -jnp.inf))
                m = (cb * seg).astype(BF16)
                y_pair.append(jnp.dot(m, xdt16, preferred_element_type=F32))
            y_d = jnp.where(lane_lo, y_pair[0], y_pair[1])
            o0 = 2 * pr * hdim
            ys.append(y_d + y_off[:, o0:o0 + 2 * hdim] * _pair_expand(e_out, h0, hdim))
            x_in.append((xdt * _pair_expand(e_in, h0, hdim)).astype(BF16))
            dec.append(_pair_expand(e_tot, h0, hdim))
        x_in = jnp.concatenate(x_in, axis=1) if len(x_in) > 1 else x_in[0]
        dec = jnp.concatenate(dec, axis=1) if len(dec) > 1 else dec[0]
        state_ref[g] = st * dec + jnp.dot(b_g.T.astype(BF16), x_in, preferred_element_type=F32)
    return jnp.concatenate(ys, axis=1)


def _ssd_fwd_kernel(xc_ref, xp_ref, xn_ref, dt_ref, cw_ref, cb_ref, bias_ref, alog_ref,
                    yf_ref, xact_ref, state_ref, *, dims):
    c = pl.program_id(1)
    nc = pl.num_programs(1)

    @pl.when(c == 0)
    def _():
        state_ref[...] = jnp.zeros_like(state_ref)

    x = xc_ref[...].astype(F32)
    q = x.shape[0]
    prev_row = jnp.where(c > 0, xp_ref[HALO_ROWS - 1:HALO_ROWS, :].astype(F32), 0.0)
    next_row = jnp.where(c < nc - 1, xn_ref[0:1, :].astype(F32), 0.0)
    rows = lax.broadcasted_iota(I32, (q, 1), 0)
    x_dn = jnp.where(rows == 0, prev_row, pltpu.roll(x, 1, 0))
    x_up = jnp.where(rows == q - 1, next_row, pltpu.roll(x, q - 1, 0))
    xc = cw_ref[0:1, :] * x_dn + cw_ref[1:2, :] * x + cw_ref[2:3, :] * x_up + cb_ref[...]
    xa = xc * jax.nn.sigmoid(xc)
    xact_ref[...] = xa.astype(BF16)
    w = dims["heads"] * dims["hdim"]
    gn = dims["groups"] * dims["nstate"]
    y = _ssd_core(xa[:, :w], xa[:, w:w + gn], xa[:, w + gn:], dt_ref[...], bias_ref[...],
                  alog_ref[...], state_ref, reverse=False, **dims)
    yf_ref[...] = y.astype(yf_ref.dtype)


def _ssd_bwd_kernel(xa_ref, dt_ref, yf_ref, z_ref, bias_ref, alog_ref, dsk_ref, ng_ref,
                    y_ref, state_ref, *, dims):
    @pl.when(pl.program_id(1) == 0)
    def _():
        state_ref[...] = jnp.zeros_like(state_ref)

    xa = xa_ref[...].astype(F32)
    w = dims["heads"] * dims["hdim"]
    gn = dims["groups"] * dims["nstate"]
    xs = xa[:, :w]
    yb = _ssd_core(xs, xa[:, w:w + gn], xa[:, w + gn:], dt_ref[...], bias_ref[...],
                   alog_ref[...], state_ref, reverse=True, **dims)
    y = yf_ref[...].astype(F32) + yb + dsk_ref[...] * xs
    z = z_ref[...].astype(F32)
    y = y * (z * jax.nn.sigmoid(z))
    gw = w // dims["groups"]
    parts = []
    for g in range(dims["groups"]):
        yg = y[:, g * gw:(g + 1) * gw]
        parts.append(yg * lax.rsqrt(jnp.mean(yg * yg, axis=-1, keepdims=True) + EPS))
    y_ref[...] = (jnp.concatenate(parts, axis=1) * ng_ref[...]).astype(y_ref.dtype)


def _pad_lanes(v):
    v = v.astype(F32)
    return jnp.pad(v, [(0, 0)] * (v.ndim - 1) + [(0, LANES - v.shape[-1])])


def ssd_mixer(zxu, dt_raw, batch, seq, conv_w, conv_b, dt_bias, a_log, d_skip, norm_g, dims):
    t = zxu.shape[0]
    w = dims["heads"] * dims["hdim"]
    xbc_w = conv_w.shape[-1]
    assert xbc_w % w == 0
    q = min(SSD_CHUNK, seq)
    nc = seq // q
    hb = q // HALO_ROWS
    nhalo = t // HALO_ROWS
    state_shape = (dims["groups"], dims["nstate"], (dims["heads"] // dims["groups"]) * dims["hdim"])
    bias = _pad_lanes(dt_bias)
    alog = _pad_lanes(a_log)
    full = lambda n: pl.BlockSpec((1, n), lambda b, c: (0, 0))
    vec = full(LANES)

    yf, xact = pl.pallas_call(
        functools.partial(_ssd_fwd_kernel, dims=dims),
        grid=(batch, nc),
        in_specs=[pl.BlockSpec((q, xbc_w), lambda b, c: (b * nc + c, 0)),
                  pl.BlockSpec((HALO_ROWS, xbc_w), lambda b, c: (jnp.maximum((b * nc + c) * hb - 1, 0), 0)),
                  pl.BlockSpec((HALO_ROWS, xbc_w), lambda b, c: (jnp.minimum((b * nc + c + 1) * hb, nhalo - 1), 0)),
                  pl.BlockSpec((q, LANES), lambda b, c: (b * nc + c, 0)),
                  pl.BlockSpec((3, xbc_w), lambda b, c: (0, 0)),
                  full(xbc_w), vec, vec],
        out_specs=[pl.BlockSpec((q, w), lambda b, c: (b * nc + c, 0)),
                   pl.BlockSpec((q, xbc_w), lambda b, c: (b * nc + c, 0))],
        out_shape=[jax.ShapeDtypeStruct((t, w), BF16), jax.ShapeDtypeStruct((t, xbc_w), BF16)],
        scratch_shapes=[pltpu.VMEM(state_shape, F32)],
        compiler_params=_cparams("parallel", "arbitrary"),
        name="ssd_forward",
    )(zxu, zxu, zxu, dt_raw, conv_w.astype(F32), conv_b.reshape(1, xbc_w).astype(F32), bias[0:1], alog[0:1])

    rev = lambda b, c: (b * nc + nc - 1 - c, 0)
    z_blk = xbc_w // w
    return pl.pallas_call(
        functools.partial(_ssd_bwd_kernel, dims=dims),
        grid=(batch, nc),
        in_specs=[pl.BlockSpec((q, xbc_w), rev),
                  pl.BlockSpec((q, LANES), rev),
                  pl.BlockSpec((q, w), rev),
                  pl.BlockSpec((q, w), lambda b, c: (b * nc + nc - 1 - c, z_blk)),
                  vec, vec, full(w), full(w)],
        out_specs=pl.BlockSpec((q, w), rev),
        out_shape=jax.ShapeDtypeStruct((t, w), BF16),
        scratch_shapes=[pltpu.VMEM(state_shape, F32)],
        compiler_params=_cparams("parallel", "arbitrary"),
        name="ssd_backward",
    )(xact, dt_raw, yf, zxu, bias[1:2], alog[1:2],
      jnp.repeat(d_skip.astype(F32), dims["hdim"]).reshape(1, w), norm_g.reshape(1, w).astype(F32))


def _s5_tables(a_re, a_im, log_step, b_re, b_im, c_re, c_im):
    qs = S5_CHUNK
    a_re, a_im, b_re, b_im, c_re, c_im = (v.astype(F32) for v in (a_re, a_im, b_re, b_im, c_re, c_im))
    _, g, p = a_re.shape
    cg = b_re.shape[-1]
    delta = jnp.exp(log_step.astype(F32))[..., None]
    mag = jnp.exp(a_re * delta)
    ar = mag * jnp.cos(a_im * delta)
    ai = mag * jnp.sin(a_im * delta)
    den = a_re * a_re + a_im * a_im
    qr = ((ar - 1.0) * a_re + ai * a_im) / den
    qi = (ai * a_re - (ar - 1.0) * a_im) / den
    bbr = qr[..., None] * b_re - qi[..., None] * b_im
    bbi = qr[..., None] * b_im + qi[..., None] * b_re
    pr, pi = [jnp.ones_like(ar)], [jnp.zeros_like(ar)]
    for _ in range(qs):
        pr.append(pr[-1] * ar - pi[-1] * ai)
        pi.append(pr[-2] * ai + pi[-1] * ar)
    pr = jnp.stack(pr)
    pi = jnp.stack(pi)
    cpr = c_re[None] * pr[:, :, :, None, :] - c_im[None] * pi[:, :, :, None, :]
    cpi = c_re[None] * pi[:, :, :, None, :] + c_im[None] * pr[:, :, :, None, :]
    wr = pr[..., None] * bbr[None] - pi[..., None] * bbi[None]
    wi = pr[..., None] * bbi[None] + pi[..., None] * bbr[None]
    kern = (jnp.einsum("tdgcp,dgpe->tdgce", cpr[:qs], bbr, precision=HIGHEST)
            - jnp.einsum("tdgcp,dgpe->tdgce", cpi[:qs], bbi, precision=HIGHEST))
    ii = jnp.arange(qs)[:, None]
    jj = jnp.arange(qs)[None, :]
    lag = ii - jj
    t_f = jnp.where((lag >= 0)[:, :, None, None, None], kern[jnp.clip(lag, 0, qs - 1), 0], 0.0)
    t_b = jnp.where((lag <= 0)[:, :, None, None, None], kern[jnp.clip(-lag, 0, qs - 1), 1], 0.0)
    toep = (t_f + t_b).transpose(2, 1, 4, 0, 3).reshape(g, qs * cg, qs * cg)
    def c_rows(d, powers):
        re = cpr[powers, d].transpose(1, 3, 0, 2)
        im = -cpi[powers, d].transpose(1, 3, 0, 2)
        return jnp.concatenate([re, im], axis=1).reshape(g, 2 * p, qs * cg)
    w_out = jnp.concatenate([toep, c_rows(0, jnp.arange(1, qs + 1)), c_rows(1, qs - jnp.arange(qs))], axis=1)
    def b_cols(d, powers):
        re = wr[powers, d].transpose(1, 0, 3, 2)
        im = wi[powers, d].transpose(1, 0, 3, 2)
        return jnp.concatenate([re, im], axis=3).reshape(g, qs * cg, 2 * p)
    w_state = jnp.concatenate([b_cols(0, qs - 1 - jnp.arange(qs)), b_cols(1, jnp.arange(qs))], axis=2)

    def multipliers(d):
        mul = jnp.concatenate([pr[qs, d], pr[qs, d]], axis=-1).reshape(1, g * 2 * p)
        swp = jnp.concatenate([-pi[qs, d], pi[qs, d]], axis=-1).reshape(1, g * 2 * p)
        return mul, swp
    return w_state.astype(BF16), w_out.astype(BF16), multipliers(0) + multipliers(1)


S5_GROUP_BLOCK = 8


def _s5_state_kernel(u_ref, w_ref, vf_ref, vb_ref):
    half = w_ref.shape[2] // 2
    for k in range(u_ref.shape[0]):
        v = jnp.dot(u_ref[k], w_ref[k], preferred_element_type=F32)
        vf_ref[:, k * half:(k + 1) * half] = v[:, :half]
        vb_ref[:, k * half:(k + 1) * half] = v[:, half:]


def _s5_scan_kernel(v_ref, mul_ref, swp_ref, o_ref, s_scr, *, reverse):
    @pl.when(pl.program_id(1) == 0)
    def _():
        s_scr[...] = jnp.zeros_like(s_scr)

    s = s_scr[...]
    mul = mul_ref[...]
    swp = swp_ref[...]
    steps = v_ref.shape[1]
    width = s.shape[1]
    re_lane = (lax.broadcasted_iota(I32, (1, width), 1) % LANES) < (LANES // 2)
    order = range(steps - 1, -1, -1) if reverse else range(steps)
    for k in order:
        o_ref[:, k, :] = s
        partner = jnp.where(re_lane, pltpu.roll(s, width - LANES // 2, 1), pltpu.roll(s, LANES // 2, 1))
        s = mul * s + swp * partner + v_ref[:, k, :]
    s_scr[...] = s


def _s5_out_kernel(u_ref, sf_ref, sb_ref, w_ref, y_ref):
    qc = u_ref.shape[2]
    ns = (w_ref.shape[1] - qc) // 2
    for k in range(u_ref.shape[0]):
        sf = sf_ref[:, k * ns:(k + 1) * ns].astype(BF16)
        sb = sb_ref[:, k * ns:(k + 1) * ns].astype(BF16)
        y = jnp.dot(u_ref[k], w_ref[k, :qc, :], preferred_element_type=F32)
        y = y + jnp.dot(sf, w_ref[k, qc:qc + ns, :], preferred_element_type=F32)
        y = y + jnp.dot(sb, w_ref[k, qc + ns:, :], preferred_element_type=F32)
        y_ref[k] = y.astype(y_ref.dtype)


def _s5_post_kernel(y_ref, u_ref, d_ref, w_ref, g_ref, o_ref):
    y = y_ref[...].astype(F32) + d_ref[...] * u_ref[...].astype(F32)
    y = jax.nn.gelu(y)
    gate = jax.nn.sigmoid(jnp.dot(y.astype(BF16), w_ref[...].astype(BF16), preferred_element_type=F32))
    o_ref[...] = _rms(y * gate, g_ref[...]).astype(o_ref.dtype)


def _group_lane_blocks(n, cg):
    return lax.broadcasted_iota(I32, (n, LANES), 1) // cg


def _s5_split_kernel(u_ref, o_ref, *, groups, cg):
    n = u_ref.shape[0]
    per_vreg = LANES // cg
    blk = _group_lane_blocks(n, cg)
    for v in range(groups // per_vreg):
        rolled = {}
        for j in range(S5_CHUNK):
            src = u_ref[:, j, v * LANES:(v + 1) * LANES].astype(F32)
            for k in range(per_vreg):
                rolled[j, k] = src if k == 0 else pltpu.roll(src, k * cg, 1)
        for gl in range(per_vreg):
            for h in range(S5_CHUNK // per_vreg):
                acc = rolled[h * per_vreg, (-gl) % per_vreg]
                for jj in range(1, per_vreg):
                    acc = jnp.where(blk == jj, rolled[h * per_vreg + jj, (jj - gl) % per_vreg], acc)
                o_ref[v * per_vreg + gl, :, h * LANES:(h + 1) * LANES] = acc.astype(o_ref.dtype)


def _s5_merge_kernel(y_ref, o_ref, *, groups, cg):
    n = y_ref.shape[1]
    per_vreg = LANES // cg
    blk = _group_lane_blocks(n, cg)
    for v in range(groups // per_vreg):
        rolled = {}
        for gl in range(per_vreg):
            for h in range(S5_CHUNK // per_vreg):
                src = y_ref[v * per_vreg + gl, :, h * LANES:(h + 1) * LANES].astype(F32)
                for k in range(per_vreg):
                    rolled[gl, h, k] = src if k == 0 else pltpu.roll(src, k * cg, 1)
        for i in range(S5_CHUNK):
            h, ii = divmod(i, per_vreg)
            acc = rolled[0, h, (-ii) % per_vreg]
            for gl in range(1, per_vreg):
                acc = jnp.where(blk == gl, rolled[gl, h, (gl - ii) % per_vreg], acc)
            o_ref[:, i, v * LANES:(v + 1) * LANES] = acc.astype(o_ref.dtype)


def s5_mixer(zxu, u_blk, batch, seq, tables, d_skip, w_glu, norm_g):
    w_state, w_out, (mul_f, swp_f, mul_b, swp_b) = tables
    t = zxu.shape[0]
    g, qc, ns2 = w_state.shape
    ns = ns2 // 2
    cg = qc // S5_CHUNK
    width = g * cg
    nchunk = seq // S5_CHUNK
    rows = nchunk * batch
    u_rows = zxu.reshape(rows, S5_CHUNK, zxu.shape[1])
    rn = min(64, rows)
    ug = pl.pallas_call(
        functools.partial(_s5_split_kernel, groups=g, cg=cg),
        grid=(rows // rn,),
        in_specs=[pl.BlockSpec((rn, S5_CHUNK, width), lambda r: (r, 0, u_blk))],
        out_specs=pl.BlockSpec((g, rn, qc), lambda r: (0, r, 0)),
        out_shape=jax.ShapeDtypeStruct((g, rows, qc), BF16),
        compiler_params=_cparams("parallel"),
        name="s5_split",
    )(u_rows)
    tr = _tile(rows, 1024)
    gb = min(S5_GROUP_BLOCK, g)
    vf, vb = pl.pallas_call(
        _s5_state_kernel,
        grid=(g // gb, rows // tr),
        in_specs=[pl.BlockSpec((gb, tr, qc), lambda i, r: (i, r, 0)),
                  pl.BlockSpec((gb, qc, ns2), lambda i, r: (i, 0, 0))],
        out_specs=[pl.BlockSpec((tr, gb * ns), lambda i, r: (r, i))] * 2,
        out_shape=[jax.ShapeDtypeStruct((rows, g * ns), F32)] * 2,
        compiler_params=_cparams("parallel", "parallel"),
        name="s5_state_in",
    )(ug, w_state)

    panel = _tile(g * ns, 4096)
    steps = min(32, nchunk)
    nblk = nchunk // steps

    def scan(v, mul, swp, reverse):
        rmap = (lambda pnl, i: (0, nblk - 1 - i, pnl)) if reverse else (lambda pnl, i: (0, i, pnl))
        return pl.pallas_call(
            functools.partial(_s5_scan_kernel, reverse=reverse),
            grid=(g * ns // panel, nblk),
            in_specs=[pl.BlockSpec((batch, steps, panel), rmap),
                      pl.BlockSpec((1, panel), lambda pnl, i: (0, pnl)),
                      pl.BlockSpec((1, panel), lambda pnl, i: (0, pnl))],
            out_specs=pl.BlockSpec((batch, steps, panel), rmap),
            out_shape=jax.ShapeDtypeStruct((batch, nchunk, g * ns), F32),
            scratch_shapes=[pltpu.VMEM((batch, panel), F32)],
            compiler_params=_cparams("parallel", "arbitrary"),
            name="s5_scan_bwd" if reverse else "s5_scan_fwd",
        )(v.reshape(batch, nchunk, g * ns), mul, swp).reshape(rows, g * ns)

    sf = scan(vf, mul_f, swp_f, False)
    sb = scan(vb, mul_b, swp_b, True)
    yg = pl.pallas_call(
        _s5_out_kernel,
        grid=(g // gb, rows // tr),
        in_specs=[pl.BlockSpec((gb, tr, qc), lambda i, r: (i, r, 0)),
                  pl.BlockSpec((tr, gb * ns), lambda i, r: (r, i)),
                  pl.BlockSpec((tr, gb * ns), lambda i, r: (r, i)),
                  pl.BlockSpec((gb, qc + 2 * ns, qc), lambda i, r: (i, 0, 0))],
        out_specs=pl.BlockSpec((gb, tr, qc), lambda i, r: (i, r, 0)),
        out_shape=jax.ShapeDtypeStruct((g, rows, qc), BF16),
        compiler_params=_cparams("parallel", "parallel"),
        name="s5_state_out",
    )(ug, sf, sb, w_out)
    y_rows = pl.pallas_call(
        functools.partial(_s5_merge_kernel, groups=g, cg=cg),
        grid=(rows // rn,),
        in_specs=[pl.BlockSpec((g, rn, qc), lambda r: (0, r, 0))],
        out_specs=pl.BlockSpec((rn, S5_CHUNK, width), lambda r: (r, 0, 0)),
        out_shape=jax.ShapeDtypeStruct((rows, S5_CHUNK, width), F32),
        compiler_params=_cparams("parallel"),
        name="s5_merge",
    )(yg)
    y = y_rows.reshape(t, width)
    tm = _tile(t, 1024)
    return pl.pallas_call(
        _s5_post_kernel,
        grid=(t // tm,),
        in_specs=[pl.BlockSpec((tm, width), lambda i: (i, 0)),
                  pl.BlockSpec((tm, width), lambda i: (i, u_blk)),
                  pl.BlockSpec((1, width), lambda i: (0, 0)),
                  pl.BlockSpec((width, width), lambda i: (0, 0)),
                  pl.BlockSpec((1, width), lambda i: (0, 0))],
        out_specs=pl.BlockSpec((tm, width), lambda i: (i, 0)),
        out_shape=jax.ShapeDtypeStruct((t, width), BF16),
        compiler_params=_cparams("parallel"),
        name="s5_post",
    )(y, zxu, d_skip.reshape(1, width).astype(F32), w_glu.astype(F32), norm_g.reshape(1, width).astype(F32))


def _attn_kernel(q_ref, kv_ref, o_ref, *, heads):
    d = q_ref.shape[1]
    hd = d // heads
    scale = hd ** -0.5
    for h in range(heads):
        qh = q_ref[:, h * hd:(h + 1) * hd]
        kh = kv_ref[:, h * hd:(h + 1) * hd]
        vh = kv_ref[:, d + h * hd:d + (h + 1) * hd]
        s = lax.dot_general(qh, kh, (((1,), (1,)), ((), ())), preferred_element_type=F32) * scale
        p = jnp.exp(s - jnp.max(s, axis=-1, keepdims=True))
        o = jnp.dot(p.astype(BF16), vh, preferred_element_type=F32) / jnp.sum(p, axis=-1, keepdims=True)
        o_ref[:, h * hd:(h + 1) * hd] = o.astype(o_ref.dtype)


def mem_attention(q, kv, batch, seq, mem_len):
    t, d = q.shape
    tq = _tile(seq, 1024)
    nq = seq // tq
    return pl.pallas_call(
        functools.partial(_attn_kernel, heads=MEM_HEADS),
        grid=(batch, nq),
        in_specs=[pl.BlockSpec((tq, d), lambda b, i: (b * nq + i, 0)),
                  pl.BlockSpec((mem_len, 2 * d), lambda b, i: (b, 0))],
        out_specs=pl.BlockSpec((tq, d), lambda b, i: (b * nq + i, 0)),
        out_shape=jax.ShapeDtypeStruct((t, d), BF16),
        compiler_params=_cparams("parallel", "arbitrary"),
        name="mem_attention",
    )(q, kv)


def _router_kernel(x_ref, g_ref, wr_ref, hn_ref, p_ref):
    hn = _rms(x_ref[...], g_ref[...])
    hn_hi = hn.astype(BF16)
    hn_ref[...] = hn_hi
    hn_lo = (hn - hn_hi.astype(F32)).astype(BF16)
    w = wr_ref[...]
    w_hi = w.astype(BF16)
    w_lo = (w - w_hi.astype(F32)).astype(BF16)
    nt = lambda a, b: lax.dot_general(a, b, (((1,), (1,)), ((), ())), preferred_element_type=F32)
    logits = nt(w_hi, hn_hi) + nt(w_lo, hn_hi) + nt(w_hi, hn_lo)
    e = jnp.exp(logits - jnp.max(logits, axis=0, keepdims=True))
    p_ref[...] = e / jnp.sum(e, axis=0, keepdims=True)


def router(x, g, w_router, tm=1024):
    t, d = x.shape
    e = w_router.shape[1]
    tm = _tile(t, tm)
    return pl.pallas_call(
        _router_kernel,
        grid=(t // tm,),
        in_specs=[pl.BlockSpec((tm, d), lambda i: (i, 0)),
                  pl.BlockSpec((1, d), lambda i: (0, 0)),
                  pl.BlockSpec((e, d), lambda i: (0, 0))],
        out_specs=[pl.BlockSpec((tm, d), lambda i: (i, 0)),
                   pl.BlockSpec((e, tm), lambda i: (0, i))],
        out_shape=[jax.ShapeDtypeStruct((t, d), BF16), jax.ShapeDtypeStruct((e, t), F32)],
        compiler_params=_cparams("parallel"),
        name="router",
    )(x, g.reshape(1, d).astype(F32), w_router.T.astype(F32))


def _select_kernel(p_ref, pos_ref, gate_ref, cnt_ref, *, cap, tb):
    e, t = p_ref.shape
    bits = pltpu.bitcast(p_ref[...], I32)

    def search(i, prefix):
        cand = prefix | jnp.left_shift(jnp.int32(1), 30 - i)
        cnt = jnp.sum((bits >= cand).astype(I32), axis=1, keepdims=True)
        return jnp.where(cnt >= cap, cand, prefix)

    thr = lax.fori_loop(0, 31, search, jnp.zeros((e, 1), I32))
    need = (cap - jnp.sum((bits > thr).astype(I32), axis=1, keepdims=True)).astype(F32)
    r = lax.broadcasted_iota(I32, (tb, tb), 0)
    c = lax.broadcasted_iota(I32, (tb, tb), 1)
    incl = (r <= c).astype(BF16)
    strict = (r < c).astype(BF16)

    def block(k, carry):
        carry_eq, carry_sel = carry
        sl = pl.ds(pl.multiple_of(k * tb, tb), tb)
        p = p_ref[:, sl]
        b = pltpu.bitcast(p, I32)
        eq = b == thr
        eq16 = jnp.where(eq, 1.0, 0.0).astype(BF16)
        rank = jnp.dot(eq16, incl, preferred_element_type=F32) + carry_eq
        sel = (b > thr) | (eq & (rank <= need))
        sel16 = jnp.where(sel, 1.0, 0.0).astype(BF16)
        pos = jnp.dot(sel16, strict, preferred_element_type=F32) + carry_sel
        pos_ref[:, sl] = jnp.where(sel, pos, -1.0)
        gate_ref[:, sl] = jnp.where(sel, p, 0.0)
        cnt_ref[k] = jnp.broadcast_to(carry_sel, (e, LANES))
        return (carry_eq + jnp.sum(eq16.astype(F32), axis=1, keepdims=True),
                carry_sel + jnp.sum(sel16.astype(F32), axis=1, keepdims=True))

    lax.fori_loop(0, t // tb, block, (jnp.zeros((e, 1), F32), jnp.zeros((e, 1), F32)))


def select_topk(probs_t, cap, tb):
    e, t = probs_t.shape
    nb = t // tb
    return pl.pallas_call(
        functools.partial(_select_kernel, cap=cap, tb=tb),
        grid=(1,),
        in_specs=[pl.BlockSpec((e, t), lambda i: (0, 0))],
        out_specs=[pl.BlockSpec((e, t), lambda i: (0, 0)),
                   pl.BlockSpec((e, t), lambda i: (0, 0)),
                   pl.BlockSpec((nb, e, LANES), lambda i: (0, 0, 0))],
        out_shape=[jax.ShapeDtypeStruct((e, t), F32), jax.ShapeDtypeStruct((e, t), F32),
                   jax.ShapeDtypeStruct((nb, e, LANES), F32)],
        compiler_params=_cparams("arbitrary"),
        name="select_topk",
    )(probs_t)


GATHER_WINDOW = 192


GATHER_EXPERTS = 4


def _gather_kernel(a0_ref, nw_ref, pos_ref, hn_ref, xe_ref, *, cap, win, ge):
    grp = pl.program_id(0)
    kb = pl.program_id(2)
    nkb = pl.num_programs(2)

    @pl.when(kb == 0)
    def _():
        xe_ref[...] = jnp.zeros_like(xe_ref)

    tb = hn_ref.shape[0]
    local = lax.broadcasted_iota(I32, (win, tb), 0).astype(F32)
    experts = [grp * ge + k for k in range(ge)]
    items = [ex * nkb + kb for ex in experts]
    windows = nw_ref[items[0]]
    for it in items[1:]:
        windows = jnp.maximum(windows, nw_ref[it])

    def window(w, carry):
        hits, bases = [], []
        for ex, it in zip(experts, items):
            slot = pos_ref[pl.ds(ex, 1), :]
            start = a0_ref[it] * ROW_ALIGN + w * win
            base = jnp.minimum(a0_ref[it] + w * (win // ROW_ALIGN), (cap - win) // ROW_ALIGN) * ROW_ALIGN
            hit = (slot - base.astype(F32) == local) & (slot >= start.astype(F32))
            hits.append(jnp.where(hit, 1.0, 0.0).astype(BF16))
            bases.append(base)
        rows = jnp.dot(jnp.concatenate(hits, axis=0), hn_ref[...], preferred_element_type=F32)
        for k, base in enumerate(bases):
            sl = pl.ds(pl.multiple_of(k * cap + base, ROW_ALIGN), win)
            xe_ref[sl, :] = xe_ref[sl, :] + rows[k * win:(k + 1) * win].astype(xe_ref.dtype)
        return carry

    lax.fori_loop(0, windows, window, 0)


def moe_gather(hn, pos_t, bounds, cap, tb):
    t, d = hn.shape
    e = pos_t.shape[0]
    win = min(GATHER_WINDOW, cap)
    assert cap % ROW_ALIGN == 0 and win % ROW_ALIGN == 0
    c0, c1 = bounds[:, :-1], bounds[:, 1:]
    a0 = c0 // ROW_ALIGN
    nw = jnp.where(c1 > c0, (c1 - a0 * ROW_ALIGN + win - 1) // win, 0)
    ge = GATHER_EXPERTS
    assert e % ge == 0 and d % (ge * LANES) == 0
    dc = d // ge
    grid_spec = pltpu.PrefetchScalarGridSpec(
        num_scalar_prefetch=2,
        grid=(e // ge, ge, t // tb),
        in_specs=[pl.BlockSpec((e, tb), lambda x, c, k, a0, nw: (0, k)),
                  pl.BlockSpec((tb, dc), lambda x, c, k, a0, nw: (k, c))],
        out_specs=pl.BlockSpec((ge * cap, dc), lambda x, c, k, a0, nw: (x, c)),
    )
    return pl.pallas_call(
        functools.partial(_gather_kernel, cap=cap, win=win, ge=ge),
        grid_spec=grid_spec,
        out_shape=jax.ShapeDtypeStruct((e * cap, d), BF16),
        compiler_params=_cparams("parallel", "parallel", "arbitrary"),
        name="moe_gather",
    )(a0.reshape(-1).astype(I32), nw.reshape(-1).astype(I32), pos_t, hn)


def _ffn_kernel(xe_ref, wg_ref, wu_ref, wd_ref, ye_ref, acc_ref):
    f = pl.program_id(2)

    @pl.when(f == 0)
    def _():
        acc_ref[...] = jnp.zeros_like(acc_ref)

    xe = xe_ref[...]
    gate = jnp.dot(xe, wg_ref[0].astype(BF16), preferred_element_type=F32)
    up = jnp.dot(xe, wu_ref[0].astype(BF16), preferred_element_type=F32)
    he = (gate * jax.nn.sigmoid(gate) * up).astype(BF16)
    acc_ref[...] += jnp.dot(he, wd_ref[0].astype(BF16), preferred_element_type=F32)

    @pl.when(f == pl.num_programs(2) - 1)
    def _():
        ye_ref[...] = acc_ref[...].astype(ye_ref.dtype)


def moe_ffn(xe, w_gate, w_up, w_down, cap, ts=1024, tf=256):
    n, d = xe.shape
    e, _, ff = w_gate.shape
    ts = _tile(cap, ts)
    tf = _tile(ff, tf)
    nsb = cap // ts
    return pl.pallas_call(
        _ffn_kernel,
        grid=(e, nsb, ff // tf),
        in_specs=[pl.BlockSpec((ts, d), lambda x, s, f: (x * nsb + s, 0)),
                  pl.BlockSpec((1, d, tf), lambda x, s, f: (x, 0, f)),
                  pl.BlockSpec((1, d, tf), lambda x, s, f: (x, 0, f)),
                  pl.BlockSpec((1, tf, d), lambda x, s, f: (x, f, 0))],
        out_specs=pl.BlockSpec((ts, d), lambda x, s, f: (x * nsb + s, 0)),
        out_shape=jax.ShapeDtypeStruct((n, d), BF16),
        scratch_shapes=[pltpu.VMEM((ts, d), F32)],
        compiler_params=_cparams("parallel", "parallel", "arbitrary"),
        name="moe_ffn",
    )(xe, w_gate, w_up, w_down)


SLOT_SPLIT = 64
SCATTER_WINDOW = 128
SCATTER_WINDOWS = 8
ROW_ALIGN = 16
NO_SLOT = 1 << 24


def _scatter_kernel(kb_ref, first_ref, last_ref, real_ref, ex_ref, off_ref, base_ref, vlo_ref,
                    tok_ref, *rest, experts, final_norm):
    ye_refs = rest[:SCATTER_WINDOWS]
    x_ref, g_ref, y_ref, acc_ref = rest[SCATTER_WINDOWS:]
    i = pl.program_id(0)
    win = SCATTER_WINDOW

    @pl.when(first_ref[i] == 1)
    def _():
        acc_ref[...] = jnp.zeros_like(acc_ref)

    @pl.when(real_ref[i] == 1)
    def _():
        tok = tok_ref[...]
        kdim = tok.shape[1]
        e = experts
        row = lax.broadcasted_iota(I32, (kdim, 2 * win), 0)
        col = lax.broadcasted_iota(I32, (kdim, 2 * win), 1)
        lane = lax.broadcasted_iota(I32, (1, win), 1).astype(F32)
        pieces = []
        for k in range(SCATTER_WINDOWS):
            w = i * SCATTER_WINDOWS + k
            ek = ex_ref[w]
            sel = jnp.where((row == ek) & (col < win), float(SLOT_SPLIT), 0.0)
            sel = sel + jnp.where((row == e + ek) & (col < win), 1.0, 0.0)
            sel = sel + jnp.where((row == 2 * e + ek) & (col >= win), 1.0, 0.0)
            both = jnp.dot(tok, sel.astype(BF16), preferred_element_type=F32)
            slot = both[:, :win]
            hit = (slot - base_ref[w].astype(F32) == lane) & (slot >= vlo_ref[w].astype(F32))
            pieces.append(jnp.where(hit, both[:, win:], 0.0).astype(BF16))
        onehot = jnp.concatenate(pieces, axis=1)
        rows = jnp.concatenate([r[...] for r in ye_refs], axis=0)
        acc_ref[...] += jnp.dot(onehot, rows, preferred_element_type=F32)

    @pl.when(last_ref[i] == 1)
    def _():
        y = x_ref[...] + acc_ref[...]
        y_ref[...] = _rms(y, g_ref[...]) if final_norm else y


def _scatter_items(bounds, cap, tb):
    e, nkb1 = bounds.shape
    nkb = nkb1 - 1
    win, grp = SCATTER_WINDOW, SCATTER_WINDOWS
    c0, c1 = bounds[:, :-1].T, bounds[:, 1:].T
    a0 = (c0 // ROW_ALIGN) * ROW_ALIGN
    nwin = jnp.where(c1 > c0, (c1 - a0 + win - 1) // win, 0)
    wmax = (tb + ROW_ALIGN - 1 + win - 1) // win
    w = jnp.arange(wmax, dtype=I32)
    valid = (w[None, None, :] < nwin[:, :, None]).reshape(-1)
    start = (a0[:, :, None] + w * win).reshape(-1)
    base = jnp.minimum(start, cap - win)
    expert = jnp.broadcast_to(jnp.arange(e, dtype=I32)[None, :, None], (nkb, e, wmax)).reshape(-1)
    block = jnp.broadcast_to(jnp.arange(nkb, dtype=I32)[:, None, None], (nkb, e, wmax)).reshape(-1)
    per_block = jnp.sum(valid.reshape(nkb, -1).astype(I32), axis=1)
    items_per_block = jnp.maximum((per_block + grp - 1) // grp, 1)
    item_end = jnp.cumsum(items_per_block)
    item_start = item_end - items_per_block
    rank = jnp.cumsum(valid.astype(I32)) - 1 - (jnp.cumsum(per_block) - per_block)[block]
    max_windows = e * cap // win + (e * nkb * (win + ROW_ALIGN - 2) + win - 1) // win
    max_items = (max_windows + grp - 1) // grp + nkb
    dest = jnp.where(valid, (item_start[block] + rank // grp) * grp + rank % grp, max_items * grp)

    fill = jnp.broadcast_to(jnp.array([0, 0, NO_SLOT, 0], I32), (max_items * grp, 4))
    values = jnp.stack([expert, (expert * cap + base) // ROW_ALIGN, base, start], axis=1).astype(I32)
    placed = fill.at[dest].set(values, mode="drop")
    it = jnp.arange(max_items, dtype=I32)
    real = it < item_end[-1]
    kb = jnp.minimum(jnp.sum((it[:, None] >= item_end[None, :]).astype(I32), axis=1), nkb - 1)
    first = real & (it == item_start[kb])
    last = real & (it == item_end[kb] - 1)
    return (kb, first.astype(I32), last.astype(I32), real.astype(I32),
            placed[:, 0], placed[:, 1], placed[:, 2], placed[:, 3])


def moe_scatter_norm(ye, pos_t, gate_t, x, g_final, final_norm, bounds, cap, tb):
    t, d = x.shape
    e = pos_t.shape[0]
    assert 3 * e <= LANES and cap % SCATTER_WINDOW == 0
    pos_c = pos_t.T
    hi = jnp.floor(pos_c / SLOT_SPLIT)
    tok = jnp.concatenate([hi, pos_c - SLOT_SPLIT * hi, gate_t.T, jnp.zeros((t, LANES - 3 * e), F32)],
                          axis=1).astype(BF16)
    items = _scatter_items(bounds, cap, tb)
    grp = SCATTER_WINDOWS
    tokmap = lambda i, kb, *_: (kb[i], 0)

    def window(k):
        return pl.BlockSpec((pl.Element(SCATTER_WINDOW), pl.Element(d)),
                            lambda i, kb, f, l, r, ex, off, base, vlo: (off[i * grp + k] * ROW_ALIGN, 0))

    grid_spec = pltpu.PrefetchScalarGridSpec(
        num_scalar_prefetch=8,
        grid=(items[0].shape[0],),
        in_specs=[pl.BlockSpec((tb, LANES), tokmap)] + [window(k) for k in range(grp)]
                 + [pl.BlockSpec((tb, d), tokmap), pl.BlockSpec((1, d), lambda i, *_: (0, 0))],
        out_specs=pl.BlockSpec((tb, d), tokmap),
        scratch_shapes=[pltpu.VMEM((tb, d), F32)],
    )
    return pl.pallas_call(
        functools.partial(_scatter_kernel, experts=e, final_norm=final_norm),
        grid_spec=grid_spec,
        out_shape=jax.ShapeDtypeStruct((t, d), F32),
        compiler_params=_cparams("arbitrary"),
        name="moe_scatter_norm",
    )(*items, tok, *([ye] * grp), x, g_final.reshape(1, d).astype(F32))


def expert_choice_moe_norm(x, norm_g, w_router, w_gate, w_up, w_down, g_final, final_norm):
    t, d = x.shape
    e = w_router.shape[1]
    cap = CAPACITY_FACTOR * t // e
    sel_tb = _tile(t, 512)
    hn, probs_t = router(x, norm_g, w_router)
    pos_t, gate_t, cnt = select_topk(probs_t, cap, sel_tb)
    counts = cnt[:, :, 0].T.astype(I32)
    bounds = jnp.concatenate([counts, jnp.full((e, 1), cap, I32)], axis=1)
    g_tb = _tile(t, 1024)
    xe = moe_gather(hn, pos_t, bounds[:, ::g_tb // sel_tb], cap, g_tb)
    ye = moe_ffn(xe, w_gate, w_up, w_down, cap)
    return moe_scatter_norm(ye, pos_t, gate_t, x, g_final, final_norm, bounds, cap, sel_tb)


def _prepare(p, depth_i):
    i = depth_i
    ssd_w = p["ssd_norm"].shape[-1]
    xbc_w = p["conv_w"].shape[-1]
    heads = p["ssd_a_log"].shape[-1]
    w_in = p["w_in"][i]
    o_z, o_x, o_dt = ssd_w, ssd_w + xbc_w, ssd_w + xbc_w + heads
    w_zxu = jnp.concatenate([w_in[:, o_z:o_x], w_in[:, :o_z], w_in[:, o_dt:]], axis=1).astype(BF16)
    w_dt = _pad_lanes(w_in[:, o_x:o_dt]).astype(BF16)
    dims = dict(heads=heads, hdim=ssd_w // heads, groups=SSD_GROUPS,
                nstate=(xbc_w - ssd_w) // (2 * SSD_GROUPS))
    tables = _s5_tables(p["s5_a_re"][i], p["s5_a_im"][i], p["s5_log_step"][i], p["s5_b_re"][i],
                        p["s5_b_im"][i], p["s5_c_re"][i], p["s5_c_im"][i])
    w_out = p["w_out"][i].astype(BF16)
    return dict(
        w_zxu=w_zxu, w_dt=w_dt, dims=dims, tables=tables, u_blk=(xbc_w + ssd_w) // (w_in.shape[1] - o_dt),
        w_out_ssd=w_out[:ssd_w], w_out_s5=w_out[ssd_w:],
        w_q=p["w_q"][i].astype(BF16),
        w_kv=jnp.concatenate([p["w_k"][i], p["w_v"][i]], axis=1).astype(BF16),
        w_o=p["w_o"][i].astype(BF16),
        w_gate=p["w_gate"][i], w_up=p["w_up"][i], w_down=p["w_down"][i],
    )


def _trunk(x, mem, p, prepared):
    batch, seq, d = x.shape
    mem_len = mem.shape[1]
    t = batch * seq
    x = x.reshape(t, d)
    mem = mem.reshape(batch * mem_len, d)
    depth = p["w_in"].shape[0]
    for i in range(depth):
        w = prepared[i]
        zxu, dt_raw = norm_matmul(x, p["norm_mix"][i], w["w_zxu"], BF16, w_side=w["w_dt"])
        y_ssd = ssd_mixer(zxu, dt_raw, batch, seq, p["conv_w"][i], p["conv_b"][i], p["ssd_dt_bias"][i],
                          p["ssd_a_log"][i], p["ssd_d"][i], p["ssd_norm"][i], w["dims"])
        y_s5 = s5_mixer(zxu, w["u_blk"], batch, seq, w["tables"], p["s5_d"][i], p["s5_w_glu"][i],
                        p["s5_norm"][i])
        x = matmul_residual([y_ssd, y_s5], [w["w_out_ssd"], w["w_out_s5"]], x)
        q = norm_matmul(x, p["norm_attn"][i], w["w_q"], BF16)
        kv = norm_matmul(mem, p["norm_mem"][i], w["w_kv"], BF16)
        o = mem_attention(q, kv, batch, seq, mem_len)
        x = matmul_residual([o], [w["w_o"]], x)
        x = expert_choice_moe_norm(x, p["norm_ffn"][i], p["w_router"][i], w["w_gate"], w["w_up"],
                                   w["w_down"], p["norm_final"], i == depth - 1)
    return x.reshape(batch, seq, d)


def kernel(x_prompt, x_sample, mem_prompt, mem_sample, norm_mix, w_in, conv_w, conv_b, ssd_dt_bias,
           ssd_a_log, ssd_d, ssd_norm, s5_a_re, s5_a_im, s5_log_step, s5_b_re, s5_b_im, s5_c_re,
           s5_c_im, s5_d, s5_w_glu, s5_norm, w_out, norm_attn, norm_mem, w_q, w_k, w_v, w_o,
           norm_ffn, w_router, w_gate, w_up, w_down, norm_final):
    p = dict(norm_mix=norm_mix, w_in=w_in, conv_w=conv_w, conv_b=conv_b, ssd_dt_bias=ssd_dt_bias,
             ssd_a_log=ssd_a_log, ssd_d=ssd_d, ssd_norm=ssd_norm, s5_a_re=s5_a_re, s5_a_im=s5_a_im,
             s5_log_step=s5_log_step, s5_b_re=s5_b_re, s5_b_im=s5_b_im, s5_c_re=s5_c_re,
             s5_c_im=s5_c_im, s5_d=s5_d, s5_w_glu=s5_w_glu, s5_norm=s5_norm, w_out=w_out,
             norm_attn=norm_attn, norm_mem=norm_mem, w_q=w_q, w_k=w_k, w_v=w_v, w_o=w_o,
             norm_ffn=norm_ffn, w_router=w_router, w_gate=w_gate, w_up=w_up, w_down=w_down,
             norm_final=norm_final)
    prepared = [_prepare(p, i) for i in range(w_in.shape[0])]
    return (_trunk(x_prompt, mem_prompt, p, prepared), _trunk(x_sample, mem_sample, p, prepared))
```
